```python
import jax, jax.numpy as jnp
from jax import lax
import numpy as np

D_MODEL = 1024
BATCH = 4
SEQ = 8192
DEPTH = 1

CHUNK = 64
LEFT_CHUNKS = 8
BAND = LEFT_CHUNKS + 1
EPS = 1e-5
D_FF = 2816
SSD_HEADS = 16
SSD_HEAD_DIM = 64
SSD_INNER = SSD_HEADS * SSD_HEAD_DIM
SSD_GROUPS = 2
SSD_STATE = 128
CONV_WIDTH = 4
CONV_DIM = SSD_INNER + 2 * SSD_GROUPS * SSD_STATE
ATT_HEADS = 8
ATT_HEAD_DIM = 64
ATT_INNER = ATT_HEADS * ATT_HEAD_DIM
MAX_REL = 256
MIX_WIDTH = SSD_INNER + ATT_INNER
PROJ_DIM = SSD_INNER + CONV_DIM + SSD_HEADS + 3 * ATT_INNER

kernel_name = "hybrid_ssd_chunkattn_macaron_block"


def rms_norm(x, w):
    xf = x.astype(jnp.float32)
    y = xf * lax.rsqrt(jnp.mean(xf * xf, axis=-1, keepdims=True) + EPS)
    return (y * w.astype(jnp.float32)).astype(x.dtype)


def swiglu(h, w_gate, w_up, w_down):
    return (jax.nn.silu(h @ w_gate) * (h @ w_up)) @ w_down


def causal_depthwise_conv(u, w, b):
    k = w.shape[0]
    out = lax.conv_general_dilated(
        u, w.astype(u.dtype)[:, None, :], window_strides=(1,), padding=[(k - 1, 0)],
        dimension_numbers=('NWC', 'WIO', 'NWC'), feature_group_count=u.shape[-1])
    return out + b.astype(u.dtype)


def ssd_chunked(xs, dt, a, bmat, cmat, d_skip):
    bsz, seqlen, n_heads, hd = xs.shape
    n_groups, n_state = bmat.shape[-2], bmat.shape[-1]
    m = n_heads // n_groups
    nc = seqlen // CHUNK
    xf = xs.astype(jnp.float32)
    xdt = (xf * dt[..., None]).reshape(bsz, nc, CHUNK, n_groups, m, hd)
    a_cs = jnp.cumsum((dt * a).reshape(bsz, nc, CHUNK, n_groups, m), axis=2)
    bc = bmat.astype(jnp.float32).reshape(bsz, nc, CHUNK, n_groups, n_state)
    cc = cmat.astype(jnp.float32).reshape(bsz, nc, CHUNK, n_groups, n_state)
    causal = jnp.tril(jnp.ones((CHUNK, CHUNK), dtype=bool))[:, :, None, None]
    seg = a_cs[:, :, :, None] - a_cs[:, :, None, :]
    decay = jnp.exp(jnp.where(causal, seg, -jnp.inf))
    cb = jnp.einsum('bclgn,bcsgn->bclsg', cc, bc)
    y_diag = jnp.einsum('bclsg,bclsgm,bcsgmp->bclgmp', cb, decay, xdt)
    decay_out = jnp.exp(a_cs[:, :, -1:] - a_cs)
    states = jnp.einsum('bclgn,bclgm,bclgmp->bcgmpn', bc, decay_out, xdt)
    chunk_decay = jnp.exp(a_cs[:, :, -1])

    def step(h, inp):
        s_c, d_c = inp
        return h * d_c[..., None, None] + s_c, h

    h0 = jnp.zeros((bsz, n_groups, m, hd, n_state), jnp.float32)
    _, prev = lax.scan(step, h0, (jnp.moveaxis(states, 1, 0), jnp.moveaxis(chunk_decay, 1, 0)))
    prev = jnp.moveaxis(prev, 0, 1)
    y_off = jnp.einsum('bclgn,bcgmpn,bclgm->bclgmp', cc, prev, jnp.exp(a_cs))
    y = (y_diag + y_off).reshape(bsz, seqlen, n_heads, hd)
    return y + xf * d_skip.astype(jnp.float32)[:, None]


def chunked_rel_attention(q, k, v, rel_bias):
    bsz, seqlen, n_heads, hd = q.shape
    nc = seqlen // CHUNK
    qc = q.reshape(bsz, nc, CHUNK, n_heads, hd)
    pad = ((0, 0), (LEFT_CHUNKS * CHUNK, 0), (0, 0), (0, 0))
    kp = jnp.pad(k, pad).reshape(bsz, nc + LEFT_CHUNKS, CHUNK, n_heads, hd)
    vp = jnp.pad(v, pad).reshape(bsz, nc + LEFT_CHUNKS, CHUNK, n_heads, hd)
    kb = jnp.stack([kp[:, o:o + nc] for o in range(BAND)], axis=2).reshape(bsz, nc, BAND * CHUNK, n_heads, hd)
    vb = jnp.stack([vp[:, o:o + nc] for o in range(BAND)], axis=2).reshape(bsz, nc, BAND * CHUNK, n_heads, hd)
    qpos = LEFT_CHUNKS * CHUNK + jnp.arange(CHUNK)
    kpos = jnp.arange(BAND * CHUNK)
    rel = jnp.clip(qpos[:, None] - kpos[None, :], -MAX_REL, MAX_REL) + MAX_REL
    bias = rel_bias.astype(jnp.float32)[:, rel]
    s = jnp.einsum('bcqhd,bckhd->bhcqk', qc, kb).astype(jnp.float32) * (hd ** -0.5) + bias[None, :, None]
    valid = (jnp.arange(nc)[:, None] - LEFT_CHUNKS + jnp.arange(BAND)[None, :]) >= 0
    valid = jnp.repeat(valid, CHUNK, axis=1)
    s = jnp.where(valid[None, None, :, None, :], s, jnp.finfo(jnp.float32).min)
    p = jax.nn.softmax(s, axis=-1).astype(v.dtype)
    o = jnp.einsum('bhcqk,bckhd->bcqhd', p, vb)
    return o.reshape(bsz, seqlen, n_heads * hd)


def hybrid_mixer(x, mix_norm, w_in, conv_w, conv_b, dt_bias, a_log, d_skip, ssd_norm, rel_bias, w_out):
    bsz, seqlen, _ = x.shape
    h = rms_norm(x, mix_norm)
    proj = h @ w_in
    o1 = SSD_INNER
    o2 = o1 + CONV_DIM
    o3 = o2 + SSD_HEADS
    o4 = o3 + ATT_INNER
    o5 = o4 + ATT_INNER
    z, xbc, dt_raw = proj[..., :o1], proj[..., o1:o2], proj[..., o2:o3]
    q, k, v = proj[..., o3:o4], proj[..., o4:o5], proj[..., o5:]
    xbc = jax.nn.silu(causal_depthwise_conv(xbc, conv_w, conv_b))
    gn = SSD_GROUPS * SSD_STATE
    xs = xbc[..., :SSD_INNER].reshape(bsz, seqlen, SSD_HEADS, SSD_HEAD_DIM)
    bm = xbc[..., SSD_INNER:SSD_INNER + gn].reshape(bsz, seqlen, SSD_GROUPS, SSD_STATE)
    cm = xbc[..., SSD_INNER + gn:].reshape(bsz, seqlen, SSD_GROUPS, SSD_STATE)
    dt = jax.nn.softplus(dt_raw.astype(jnp.float32) + dt_bias.astype(jnp.float32))
    a = -jnp.exp(a_log.astype(jnp.float32))
    y = ssd_chunked(xs, dt, a, bm, cm, d_skip).reshape(bsz, seqlen, SSD_INNER)
    y = rms_norm(y * jax.nn.silu(z.astype(jnp.float32)), ssd_norm).astype(x.dtype)
    att = chunked_rel_attention(
        q.reshape(bsz, seqlen, ATT_HEADS, ATT_HEAD_DIM),
        k.reshape(bsz, seqlen, ATT_HEADS, ATT_HEAD_DIM),
        v.reshape(bsz, seqlen, ATT_HEADS, ATT_HEAD_DIM), rel_bias).astype(x.dtype)
    return jnp.concatenate([y, att], axis=-1) @ w_out


def setup_inputs(seed: int = 0) -> dict:
    key = jax.random.key(seed)
    ks = jax.random.split(key, 24)

    def nrm(k, shape, scale):
        return jax.random.normal(k, shape, jnp.float32) * scale

    def gain(k, n):
        return 1.0 + nrm(k, (DEPTH, n), 0.02)

    u = jax.random.uniform(ks[10], (DEPTH, SSD_HEADS), jnp.float32)
    dt0 = jnp.exp(u * (np.log(0.1) - np.log(0.001)) + np.log(0.001))
    dt_bias = dt0 + jnp.log(-jnp.expm1(-dt0))
    a_log = jnp.log(jax.random.uniform(ks[11], (DEPTH, SSD_HEADS), jnp.float32, 1.0, 16.0))
    return {
        "x": nrm(ks[0], (BATCH, SEQ, D_MODEL), 1.0),
        "ffn1_norm": gain(ks[1], D_MODEL),
        "ffn1_w_gate": nrm(ks[2], (DEPTH, D_MODEL, D_FF), D_MODEL ** -0.5),
        "ffn1_w_up": nrm(ks[3], (DEPTH, D_MODEL, D_FF), D_MODEL ** -0.5),
        "ffn1_w_down": nrm(ks[4], (DEPTH, D_FF, D_MODEL), D_FF ** -0.5),
        "mix_norm": gain(ks[5], D_MODEL),
        "w_in": nrm(ks[6], (DEPTH, D_MODEL, PROJ_DIM), D_MODEL ** -0.5),
        "conv_w": nrm(ks[7], (DEPTH, CONV_WIDTH, CONV_DIM), CONV_WIDTH ** -0.5),
        "conv_b": nrm(ks[8], (DEPTH, CONV_DIM), 0.02),
        "dt_bias": dt_bias,
        "a_log": a_log,
        "d_skip": 1.0 + nrm(ks[12], (DEPTH, SSD_HEADS), 0.1),
        "ssd_norm": gain(ks[13], SSD_INNER),
        "rel_bias": nrm(ks[14], (DEPTH, ATT_HEADS, 2 * MAX_REL + 1), 0.1),
        "w_out": nrm(ks[15], (DEPTH, MIX_WIDTH, D_MODEL), MIX_WIDTH ** -0.5),
        "ffn2_norm": gain(ks[16], D_MODEL),
        "ffn2_w_gate": nrm(ks[17], (DEPTH, D_MODEL, D_FF), D_MODEL ** -0.5),
        "ffn2_w_up": nrm(ks[18], (DEPTH, D_MODEL, D_FF), D_MODEL ** -0.5),
        "ffn2_w_down": nrm(ks[19], (DEPTH, D_FF, D_MODEL), D_FF ** -0.5),
        "final_norm": 1.0 + nrm(ks[20], (D_MODEL,), 0.02),
    }


def reference(x, ffn1_norm, ffn1_w_gate, ffn1_w_up, ffn1_w_down, mix_norm, w_in, conv_w, conv_b,
              dt_bias, a_log, d_skip, ssd_norm, rel_bias, w_out, ffn2_norm, ffn2_w_gate, ffn2_w_up,
              ffn2_w_down, final_norm):
    for i in range(DEPTH):
        x = x + 0.5 * swiglu(rms_norm(x, ffn1_norm[i]), ffn1_w_gate[i], ffn1_w_up[i], ffn1_w_down[i])
        x = x + hybrid_mixer(x, mix_norm[i], w_in[i], conv_w[i], conv_b[i], dt_bias[i], a_log[i],
                             d_skip[i], ssd_norm[i], rel_bias[i], w_out[i])
        x = x + 0.5 * swiglu(rms_norm(x, ffn2_norm[i]), ffn2_w_gate[i], ffn2_w_up[i], ffn2_w_down[i])
    return rms_norm(x, final_norm)
```

```python
import functools

import jax
import jax.numpy as jnp
import numpy as np
from jax import lax
from jax.experimental import pallas as pl
from jax.experimental.pallas import tpu as pltpu

D_MODEL = 1024
CHUNK = 64
LEFT_CHUNKS = 8
EPS = 1e-5
D_FF = 2816
SSD_HEADS = 16
SSD_HEAD_DIM = 64
SSD_INNER = SSD_HEADS * SSD_HEAD_DIM
SSD_GROUPS = 2
HEADS_PER_GROUP = SSD_HEADS // SSD_GROUPS
SSD_STATE = 128
CONV_WIDTH = 4
CONV_DIM = SSD_INNER + 2 * SSD_GROUPS * SSD_STATE
ATT_HEADS = 8
ATT_HEAD_DIM = 64
ATT_INNER = ATT_HEADS * ATT_HEAD_DIM
MAX_REL = 256
MIX_WIDTH = SSD_INNER + ATT_INNER

LANES = 128
SUBLANES = 8
DT_PAD = LANES
PROJ_PAD = SSD_INNER + CONV_DIM + 3 * ATT_INNER + DT_PAD

TM_FFN1 = 256
TM_FFN2 = 512
T_SSD = 256
TQ_ATT = 256
KV_BLOCKS = 3
F_CHUNKS = ((0, 1024), (1024, 2048), (2048, D_FF))
VMEM_LIMIT = 56 * 1024 * 1024

F32 = jnp.float32
BF16 = jnp.bfloat16


def _rms(x, w):
    return x * lax.rsqrt(jnp.mean(x * x, axis=-1, keepdims=True) + EPS) * w


def _silu(x):
    return x * (1.0 / (1.0 + jnp.exp(-x)))


def _swiglu(h, wg_ref, wu_ref, wd_ref, act_ref):
    for lo, hi in F_CHUNKS:
        g = jnp.dot(h, wg_ref[:, lo:hi], preferred_element_type=F32)
        u = jnp.dot(h, wu_ref[:, lo:hi], preferred_element_type=F32)
        act_ref[:, lo:hi] = (_silu(g) * u).astype(BF16)
    return jnp.dot(act_ref[...], wd_ref[...], preferred_element_type=F32)


def _ffn1_proj_kernel(x_ref, n1_ref, wg_ref, wu_ref, wd_ref, nm_ref, win_ref,
                      x1_ref, z_ref, xbc_ref, q_ref, k_ref, v_ref, dt_ref, act_ref):
    x = x_ref[...]
    h = _rms(x, n1_ref[...]).astype(BF16)
    x1 = x + 0.5 * _swiglu(h, wg_ref, wu_ref, wd_ref, act_ref)
    x1_ref[...] = x1
    h2 = _rms(x1, nm_ref[...]).astype(BF16)
    o = 0
    for ref in (z_ref, xbc_ref, q_ref, k_ref, v_ref, dt_ref):
        n = ref.shape[-1]
        ref[...] = jnp.dot(h2, win_ref[:, o:o + n], preferred_element_type=F32).astype(ref.dtype)
        o += n


def _const_spec(shape):
    return pl.BlockSpec(shape, lambda *_: (0,) * len(shape), pipeline_mode=pl.Buffered(1))


def _ffn1_proj(x, n1, wg, wu, wd, nm, win):
    m = x.shape[0]
    tm = TM_FFN1
    row = lambda n: pl.BlockSpec((tm, n), lambda i: (i, 0))
    return pl.pallas_call(
        _ffn1_proj_kernel,
        grid=(m // tm,),
        in_specs=[row(D_MODEL), _const_spec((1, D_MODEL)), _const_spec((D_MODEL, D_FF)),
                  _const_spec((D_MODEL, D_FF)), _const_spec((D_FF, D_MODEL)),
                  _const_spec((1, D_MODEL)), _const_spec((D_MODEL, PROJ_PAD))],
        out_specs=[row(D_MODEL), row(SSD_INNER), row(CONV_DIM), row(ATT_INNER), row(ATT_INNER),
                   row(ATT_INNER), row(DT_PAD)],
        out_shape=[jax.ShapeDtypeStruct((m, D_MODEL), F32),
                   jax.ShapeDtypeStruct((m, SSD_INNER), F32),
                   jax.ShapeDtypeStruct((m, CONV_DIM), F32),
                   jax.ShapeDtypeStruct((m, ATT_INNER), BF16),
                   jax.ShapeDtypeStruct((m, ATT_INNER), BF16),
                   jax.ShapeDtypeStruct((m, ATT_INNER), BF16),
                   jax.ShapeDtypeStruct((m, DT_PAD), F32)],
        scratch_shapes=[pltpu.VMEM((tm, D_FF), BF16)],
        compiler_params=pltpu.CompilerParams(dimension_semantics=("parallel",),
                                             vmem_limit_bytes=VMEM_LIMIT),
        name="ffn1_proj",
    )(x, n1, wg, wu, wd, nm, win)


def _ssd_kernel(xbc_ref, dt_ref, z_ref, cw_ref, cb_ref, dtb_ref, a_ref, dsk_ref, nw_ref,
                y_ref, buf_ref, h_ref, yacc_ref):
    t = T_SSD
    lt = pl.program_id(1)

    @pl.when(lt == 0)
    def _():
        buf_ref[0:SUBLANES, :] = jnp.zeros((SUBLANES, CONV_DIM), F32)
        h_ref[...] = jnp.zeros_like(h_ref)

    @pl.when(lt != 0)
    def _():
        buf_ref[0:SUBLANES, :] = buf_ref[t:t + SUBLANES, :]

    buf_ref[SUBLANES:SUBLANES + t, :] = xbc_ref[0]

    acc = cb_ref[...]
    for k in range(CONV_WIDTH):
        off = SUBLANES - (CONV_WIDTH - 1) + k
        acc = acc + buf_ref[off:off + t, :] * cw_ref[k:k + 1, :]
    xbc = _silu(acc)
    gn = SSD_GROUPS * SSD_STATE
    xs = xbc[:, :SSD_INNER]
    bm = xbc[:, SSD_INNER:SSD_INNER + gn]
    cm = xbc[:, SSD_INNER + gn:]

    dt_raw_t = dt_ref[0].T
    xr = dt_raw_t + dtb_ref[...]
    dt_t = jnp.maximum(xr, 0.0) + jnp.log1p(jnp.exp(-jnp.abs(xr)))
    da_t = dt_t * a_ref[...]
    pos = lax.broadcasted_iota(jnp.int32, (DT_PAD, t), 1) % CHUNK
    acs_t = da_t
    shift = 1
    while shift < CHUNK:
        acs_t = acs_t + jnp.where(pos >= shift, pltpu.roll(acs_t, shift, 1), 0.0)
        shift *= 2
    acs_c = acs_t.T

    li = lax.broadcasted_iota(jnp.int32, (CHUNK, CHUNK), 0)
    si = lax.broadcasted_iota(jnp.int32, (CHUNK, CHUNK), 1)
    causal = li >= si
    bm_bf = bm.astype(BF16)
    cm_bf = cm.astype(BF16)
    xs_bf = xs.astype(BF16)
    bt = [bm[:, g * SSD_STATE:(g + 1) * SSD_STATE].T for g in range(SSD_GROUPS)]

    for c in range(t // CHUNK):
        r0 = c * CHUNK
        for g in range(SSD_GROUPS):
            n0 = g * SSD_STATE
            cg = cm_bf[r0:r0 + CHUNK, n0:n0 + SSD_STATE]
            bg = bm_bf[r0:r0 + CHUNK, n0:n0 + SSD_STATE]
            cbm = lax.dot_general(cg, bg, (((1,), (1,)), ((), ())), preferred_element_type=F32)
            btg = bt[g][:, r0:r0 + CHUNK]
            hprev = h_ref[g]
            yoff = jnp.dot(cg, hprev.astype(BF16), preferred_element_type=F32)
            hnew = []
            for m in range(HEADS_PER_GROUP):
                hd = g * HEADS_PER_GROUP + m
                p0 = hd * SSD_HEAD_DIM
                col = acs_c[r0:r0 + CHUNK, hd:hd + 1]
                row = acs_t[hd:hd + 1, r0:r0 + CHUNK]
                dtr = dt_t[hd:hd + 1, r0:r0 + CHUNK]
                last = acs_t[hd:hd + 1, r0 + CHUNK - 1:r0 + CHUNK]
                decay = jnp.exp(jnp.where(causal, col - row, -jnp.inf))
                mh = (cbm * decay * dtr).astype(BF16)
                xh = xs_bf[r0:r0 + CHUNK, p0:p0 + SSD_HEAD_DIM]
                yd = jnp.dot(mh, xh, preferred_element_type=F32)
                yo = yoff[:, m * SSD_HEAD_DIM:(m + 1) * SSD_HEAD_DIM] * jnp.exp(col)
                yacc_ref[r0:r0 + CHUNK, p0:p0 + SSD_HEAD_DIM] = yd + yo
                w_row = jnp.exp(last - row) * dtr
                st = jnp.dot((btg * w_row).astype(BF16), xh, preferred_element_type=F32)
                hnew.append(hprev[:, m * SSD_HEAD_DIM:(m + 1) * SSD_HEAD_DIM] * jnp.exp(last) + st)
            h_ref[g] = jnp.concatenate(hnew, axis=1)

    y = yacc_ref[...] + xs * dsk_ref[...]
    y = y * _silu(z_ref[0])
    y_ref[0] = _rms(y, nw_ref[...]).astype(y_ref.dtype)


def _ssd(xbc, dt, z, cw, cb, dtb, a, dsk, nw):
    bsz, seqlen, _ = xbc.shape
    t = T_SSD
    tok = lambda n: pl.BlockSpec((1, t, n), lambda b, i: (b, i, 0))
    par = lambda shape: pl.BlockSpec(shape, lambda b, i: (0,) * len(shape))
    return pl.pallas_call(
        _ssd_kernel,
        grid=(bsz, seqlen // t),
        in_specs=[tok(CONV_DIM), tok(DT_PAD), tok(SSD_INNER), par((CONV_WIDTH, CONV_DIM)),
                  par((1, CONV_DIM)), par((DT_PAD, 1)), par((DT_PAD, 1)), par((1, SSD_INNER)),
                  par((1, SSD_INNER))],
        out_specs=tok(SSD_INNER),
        out_shape=jax.ShapeDtypeStruct((bsz, seqlen, SSD_INNER), BF16),
        scratch_shapes=[pltpu.VMEM((t + SUBLANES, CONV_DIM), F32),
                        pltpu.VMEM((SSD_GROUPS, SSD_STATE, HEADS_PER_GROUP * SSD_HEAD_DIM), F32),
                        pltpu.VMEM((t, SSD_INNER), F32)],
        compiler_params=pltpu.CompilerParams(dimension_semantics=("parallel", "arbitrary"),
                                             vmem_limit_bytes=VMEM_LIMIT),
        name="ssd",
    )(xbc, dt, z, cw, cb, dtb, a, dsk, nw)


def _attn_kernel(q_ref, k0_ref, k1_ref, k2_ref, v0_ref, v1_ref, v2_ref, bias_ref, o_ref):
    lt = pl.program_id(1)
    tq = TQ_ATT
    kpos = lax.broadcasted_iota(jnp.int32, (tq, KV_BLOCKS * tq), 1) + (lt - (KV_BLOCKS - 1)) * tq
    started = kpos >= 0
    scale = ATT_HEAD_DIM ** -0.5
    neg = jnp.finfo(F32).min
    outs = []
    for hd in range(ATT_HEADS):
        d0 = hd * ATT_HEAD_DIM
        qh = q_ref[0, :, d0:d0 + ATT_HEAD_DIM]
        s = jnp.concatenate(
            [lax.dot_general(qh, kr[0, :, d0:d0 + ATT_HEAD_DIM], (((1,), (1,)), ((), ())),
                             preferred_element_type=F32) for kr in (k0_ref, k1_ref, k2_ref)], axis=1)
        s = jnp.where(started, s * scale + bias_ref[hd], neg)
        p = jnp.exp(s - jnp.max(s, axis=-1, keepdims=True))
        denom = jnp.sum(p, axis=-1, keepdims=True)
        pb = p.astype(BF16)
        o = jnp.zeros((tq, ATT_HEAD_DIM), F32)
        for j, vr in enumerate((v0_ref, v1_ref, v2_ref)):
            o = o + jnp.dot(pb[:, j * tq:(j + 1) * tq], vr[0, :, d0:d0 + ATT_HEAD_DIM],
                            preferred_element_type=F32)
        outs.append(o / denom)
    o_ref[0] = jnp.concatenate(outs, axis=1).astype(o_ref.dtype)


def _attn_bias_table(rel_bias):
    tq = TQ_ATT
    qpos = (KV_BLOCKS - 1) * tq + np.arange(tq)
    kpos = np.arange(KV_BLOCKS * tq)
    rel = np.clip(qpos[:, None] - kpos[None, :], -MAX_REL, MAX_REL) + MAX_REL
    dchunk = qpos[:, None] // CHUNK - kpos[None, :] // CHUNK
    band = (dchunk >= 0) & (dchunk <= LEFT_CHUNKS)
    bias = rel_bias.astype(F32)[:, rel]
    return jnp.where(band[None], bias, jnp.finfo(F32).min)


def _attn(q, k, v, bias):
    bsz, seqlen, _ = q.shape
    tq = TQ_ATT
    blk = lambda back: pl.BlockSpec((1, tq, ATT_INNER), lambda b, i: (b, jnp.maximum(i - back, 0), 0))
    return pl.pallas_call(
        _attn_kernel,
        grid=(bsz, seqlen // tq),
        in_specs=[blk(0), blk(2), blk(1), blk(0), blk(2), blk(1), blk(0),
                  _const_spec((ATT_HEADS, tq, KV_BLOCKS * tq))],
        out_specs=blk(0),
        out_shape=jax.ShapeDtypeStruct((bsz, seqlen, ATT_INNER), BF16),
        compiler_params=pltpu.CompilerParams(dimension_semantics=("parallel", "parallel"),
                                             vmem_limit_bytes=VMEM_LIMIT),
        name="attn",
    )(q, k, k, k, v, v, v, bias)


def _out_ffn2_kernel(x1_ref, y_ref, att_ref, wo_ref, n2_ref, wg_ref, wu_ref, wd_ref, nf_ref,
                     out_ref, act_ref):
    mix = jnp.dot(y_ref[...], wo_ref[0:SSD_INNER, :], preferred_element_type=F32)
    mix = mix + jnp.dot(att_ref[...], wo_ref[SSD_INNER:MIX_WIDTH, :], preferred_element_type=F32)
    x2 = x1_ref[...] + mix
    h = _rms(x2, n2_ref[...]).astype(BF16)
    x3 = x2 + 0.5 * _swiglu(h, wg_ref, wu_ref, wd_ref, act_ref)
    out_ref[...] = _rms(x3, nf_ref[...])


def _out_ffn2(x1, y, att, wo, n2, wg, wu, wd, nf):
    m = x1.shape[0]
    tm = TM_FFN2
    row = lambda n: pl.BlockSpec((tm, n), lambda i: (i, 0))
    return pl.pallas_call(
        _out_ffn2_kernel,
        grid=(m // tm,),
        in_specs=[row(D_MODEL), row(SSD_INNER), row(ATT_INNER), _const_spec((MIX_WIDTH, D_MODEL)),
                  _const_spec((1, D_MODEL)), _const_spec((D_MODEL, D_FF)), _const_spec((D_MODEL, D_FF)),
                  _const_spec((D_FF, D_MODEL)), _const_spec((1, D_MODEL))],
        out_specs=row(D_MODEL),
        out_shape=jax.ShapeDtypeStruct((m, D_MODEL), F32),
        scratch_shapes=[pltpu.VMEM((tm, D_FF), BF16)],
        compiler_params=pltpu.CompilerParams(dimension_semantics=("parallel",),
                                             vmem_limit_bytes=VMEM_LIMIT),
        name="out_ffn2",
    )(x1, y, att, wo, n2, wg, wu, wd, nf)


def _layer(x, ffn1_norm, ffn1_w_gate, ffn1_w_up, ffn1_w_down, mix_norm, w_in, conv_w, conv_b, dt_bias,
           a_log, d_skip, ssd_norm, rel_bias, w_out, ffn2_norm, ffn2_w_gate, ffn2_w_up, ffn2_w_down,
           out_norm):
    bsz, seqlen, _ = x.shape
    m = bsz * seqlen
    o_dt = SSD_INNER + CONV_DIM
    win = jnp.concatenate(
        [w_in[:, :o_dt], w_in[:, o_dt + SSD_HEADS:], w_in[:, o_dt:o_dt + SSD_HEADS],
         jnp.zeros((D_MODEL, DT_PAD - SSD_HEADS), w_in.dtype)], axis=1).astype(BF16)
    row = lambda p: p.reshape(1, -1).astype(F32)
    pad_col = lambda p: jnp.pad(p.astype(F32), (0, DT_PAD - SSD_HEADS)).reshape(DT_PAD, 1)

    x1, z, xbc, q, k, v, dt = _ffn1_proj(
        x.reshape(m, D_MODEL), row(ffn1_norm), ffn1_w_gate.astype(BF16), ffn1_w_up.astype(BF16),
        ffn1_w_down.astype(BF16), row(mix_norm), win)

    seq = lambda arr: arr.reshape(bsz, seqlen, arr.shape[-1])
    y = _ssd(seq(xbc), seq(dt), seq(z), conv_w.astype(F32), row(conv_b), pad_col(dt_bias),
             pad_col(-jnp.exp(a_log.astype(F32))), row(jnp.repeat(d_skip, SSD_HEAD_DIM)), row(ssd_norm))
    att = _attn(seq(q), seq(k), seq(v), _attn_bias_table(rel_bias))

    out = _out_ffn2(x1, y.reshape(m, SSD_INNER), att.reshape(m, ATT_INNER), w_out.astype(BF16),
                    row(ffn2_norm), ffn2_w_gate.astype(BF16), ffn2_w_up.astype(BF16),
                    ffn2_w_down.astype(BF16), row(out_norm))
    return out.reshape(bsz, seqlen, D_MODEL)


def kernel(x, ffn1_norm, ffn1_w_gate, ffn1_w_up, ffn1_w_down, mix_norm, w_in, conv_w, conv_b, dt_bias, a_log, d_skip, ssd_norm, rel_bias, w_out, ffn2_norm, ffn2_w_gate, ffn2_w_up, ffn2_w_down, final_norm):
    depth = ffn1_norm.shape[0]
    assert depth == 1, "the fused out_ffn2 kernel applies the final norm after the only layer"
    return _layer(x, ffn1_norm[0], ffn1_w_gate[0], ffn1_w_up[0], ffn1_w_down[0], mix_norm[0], w_in[0],
                  conv_w[0], conv_b[0], dt_bias[0], a_log[0], d_skip[0], ssd_norm[0], rel_bias[0],
                  w_out[0], ffn2_norm[0], ffn2_w_gate[0], ffn2_w_up[0], ffn2_w_down[0], final_norm)
```

```python
import functools

import jax
import jax.numpy as jnp
import numpy as np
from jax import lax
from jax.experimental import pallas as pl
from jax.experimental.pallas import tpu as pltpu

D_MODEL = 1024
CHUNK = 64
LEFT_CHUNKS = 8
EPS = 1e-5
D_FF = 2816
SSD_HEADS = 16
SSD_HEAD_DIM = 64
SSD_INNER = SSD_HEADS * SSD_HEAD_DIM
SSD_GROUPS = 2
HEADS_PER_GROUP = SSD_HEADS // SSD_GROUPS
SSD_STATE = 128
CONV_WIDTH = 4
CONV_DIM = SSD_INNER + 2 * SSD_GROUPS * SSD_STATE
ATT_HEADS = 8
ATT_HEAD_DIM = 64
ATT_INNER = ATT_HEADS * ATT_HEAD_DIM
MAX_REL = 256
MIX_WIDTH = SSD_INNER + ATT_INNER

LANES = 128
SUBLANES = 8
DT_PAD = LANES
PROJ_PAD = SSD_INNER + CONV_DIM + 3 * ATT_INNER + DT_PAD

TM_FFN1 = 256
TM_FFN2 = 512
T_SSD = 256
TQ_ATT = 256
KV_BLOCKS = 3
F_CHUNKS = ((0, 1024), (1024, 2048), (2048, D_FF))
VMEM_LIMIT = 56 * 1024 * 1024

F32 = jnp.float32
BF16 = jnp.bfloat16


def _rms(x, w):
    return x * lax.rsqrt(jnp.mean(x * x, axis=-1, keepdims=True) + EPS) * w


def _silu(x):
    return x * (1.0 / (1.0 + jnp.exp(-x)))


def _swiglu(h, wg_ref, wu_ref, wd_ref, act_ref):
    for lo, hi in F_CHUNKS:
        g = jnp.dot(h, wg_ref[:, lo:hi], preferred_element_type=F32)
        u = jnp.dot(h, wu_ref[:, lo:hi], preferred_element_type=F32)
        act_ref[:, lo:hi] = (_silu(g) * u).astype(BF16)
    return jnp.dot(act_ref[...], wd_ref[...], preferred_element_type=F32)


def _ffn1_proj_kernel(x_ref, n1_ref, wg_ref, wu_ref, wd_ref, nm_ref, win_ref,
                      x1_ref, z_ref, xbc_ref, q_ref, k_ref, v_ref, dt_ref, act_ref):
    x = x_ref[...]
    h = _rms(x, n1_ref[...]).astype(BF16)
    x1 = x + 0.5 * _swiglu(h, wg_ref, wu_ref, wd_ref, act_ref)
    x1_ref[...] = x1
    h2 = _rms(x1, nm_ref[...]).astype(BF16)
    o = 0
    for ref in (z_ref, xbc_ref, q_ref, k_ref, v_ref, dt_ref):
        n = ref.shape[-1]
        ref[...] = jnp.dot(h2, win_ref[:, o:o + n], preferred_element_type=F32).astype(ref.dtype)
        o += n


def _const_spec(shape):
    return pl.BlockSpec(shape, lambda *_: (0,) * len(shape), pipeline_mode=pl.Buffered(1))


def _ffn1_proj(x, n1, wg, wu, wd, nm, win):
    m = x.shape[0]
    tm = TM_FFN1
    row = lambda n: pl.BlockSpec((tm, n), lambda i: (i, 0))
    return pl.pallas_call(
        _ffn1_proj_kernel,
        grid=(m // tm,),
        in_specs=[row(D_MODEL), _const_spec((1, D_MODEL)), _const_spec((D_MODEL, D_FF)),
                  _const_spec((D_MODEL, D_FF)), _const_spec((D_FF, D_MODEL)),
                  _const_spec((1, D_MODEL)), _const_spec((D_MODEL, PROJ_PAD))],
        out_specs=[row(D_MODEL), row(SSD_INNER), row(CONV_DIM), row(ATT_INNER), row(ATT_INNER),
                   row(ATT_INNER), row(DT_PAD)],
        out_shape=[jax.ShapeDtypeStruct((m, D_MODEL), F32),
                   jax.ShapeDtypeStruct((m, SSD_INNER), F32),
                   jax.ShapeDtypeStruct((m, CONV_DIM), F32),
                   jax.ShapeDtypeStruct((m, ATT_INNER), BF16),
                   jax.ShapeDtypeStruct((m, ATT_INNER), BF16),
                   jax.ShapeDtypeStruct((m, ATT_INNER), BF16),
                   jax.ShapeDtypeStruct((m, DT_PAD), F32)],
        scratch_shapes=[pltpu.VMEM((tm, D_FF), BF16)],
        compiler_params=pltpu.CompilerParams(dimension_semantics=("parallel",),
                                             vmem_limit_bytes=VMEM_LIMIT),
        name="ffn1_proj",
    )(x, n1, wg, wu, wd, nm, win)


def _ssd_kernel(xbc_ref, dt_ref, z_ref, cw_ref, cb_ref, dtb_ref, a_ref, dsk_ref, nw_ref,
                y_ref, buf_ref, h_ref):
    t = T_SSD
    lt = pl.program_id(1)

    @pl.when(lt == 0)
    def _():
        buf_ref[0:SUBLANES, :] = jnp.zeros((SUBLANES, CONV_DIM), F32)
        h_ref[...] = jnp.zeros_like(h_ref)

    @pl.when(lt != 0)
    def _():
        buf_ref[0:SUBLANES, :] = buf_ref[t:t + SUBLANES, :]

    buf_ref[SUBLANES:SUBLANES + t, :] = xbc_ref[0]

    acc = cb_ref[...]
    for k in range(CONV_WIDTH):
        off = SUBLANES - (CONV_WIDTH - 1) + k
        acc = acc + buf_ref[off:off + t, :] * cw_ref[k:k + 1, :]
    xbc = _silu(acc)
    gn = SSD_GROUPS * SSD_STATE
    xs = xbc[:, :SSD_INNER]
    bm = xbc[:, SSD_INNER:SSD_INNER + gn]
    cm = xbc[:, SSD_INNER + gn:]

    dt_raw_t = dt_ref[0].T
    xr = dt_raw_t + dtb_ref[...]
    dt_t = jnp.maximum(xr, 0.0) + jnp.log1p(jnp.exp(-jnp.abs(xr)))
    da_t = dt_t * a_ref[...]
    pos = lax.broadcasted_iota(jnp.int32, (DT_PAD, t), 1)
    acs_t = da_t
    shift = 1
    while shift < t:
        acs_t = acs_t + jnp.where(pos >= shift, pltpu.roll(acs_t, shift, 1), 0.0)
        shift *= 2
    acs_c = acs_t.T
    dt_c = dt_t.T

    causal = (lax.broadcasted_iota(jnp.int32, (t, t), 0) >= lax.broadcasted_iota(jnp.int32, (t, t), 1))
    first = lambda rows: lax.broadcasted_iota(jnp.int32, (rows, 2 * SSD_HEAD_DIM), 1) < SSD_HEAD_DIM
    lo, lo_st, lo_row = first(t), first(SSD_STATE), first(1)
    bm_bf = bm.astype(BF16)
    cm_bf = cm.astype(BF16)
    pw = 2 * SSD_HEAD_DIM
    ys = []
    for g in range(SSD_GROUPS):
        n0 = g * SSD_STATE
        cg = cm_bf[:, n0:n0 + SSD_STATE]
        cbm = lax.dot_general(cg, bm_bf[:, n0:n0 + SSD_STATE], (((1,), (1,)), ((), ())),
                              preferred_element_type=F32)
        bt = bm[:, n0:n0 + SSD_STATE].T
        hprev = h_ref[g]
        yoff = jnp.dot(cg, hprev.astype(BF16), preferred_element_type=F32)
        for pr in range(HEADS_PER_GROUP // 2):
            h0 = g * HEADS_PER_GROUP + 2 * pr
            h1 = h0 + 1
            l0 = pr * pw
            x_pair = xs[:, h0 * SSD_HEAD_DIM:h0 * SSD_HEAD_DIM + pw]
            xdt = (x_pair * jnp.where(lo, dt_c[:, h0:h0 + 1], dt_c[:, h1:h1 + 1])).astype(BF16)
            lhs = []
            for hd in (h0, h1):
                seg = acs_c[:, hd:hd + 1] - acs_t[hd:hd + 1, :]
                lhs.append((cbm * jnp.exp(jnp.where(causal, seg, -jnp.inf))).astype(BF16))
            lasts = [acs_t[hd:hd + 1, t - 1:t] for hd in (h0, h1)]
            for hd, last in zip((h0, h1), lasts):
                lhs.append((bt * jnp.exp(last - acs_t[hd:hd + 1, :])).astype(BF16))
            r = jnp.dot(jnp.concatenate(lhs, axis=0), xdt, preferred_element_type=F32)
            yd = jnp.where(lo, r[0:t], r[t:2 * t])
            st = jnp.where(lo_st, r[2 * t:2 * t + SSD_STATE], r[2 * t + SSD_STATE:])
            ecol = jnp.exp(jnp.where(lo, acs_c[:, h0:h0 + 1], acs_c[:, h1:h1 + 1]))
            ys.append(yd + yoff[:, l0:l0 + pw] * ecol)
            elast = jnp.exp(jnp.where(lo_row, lasts[0], lasts[1]))
            h_ref[g, :, l0:l0 + pw] = hprev[:, l0:l0 + pw] * elast + st

    y = jnp.concatenate(ys, axis=1) + xs * dsk_ref[...]
    y = y * _silu(z_ref[0])
    y_ref[0] = _rms(y, nw_ref[...]).astype(y_ref.dtype)


def _ssd(xbc, dt, z, cw, cb, dtb, a, dsk, nw):
    bsz, seqlen, _ = xbc.shape
    t = T_SSD
    tok = lambda n: pl.BlockSpec((1, t, n), lambda b, i: (b, i, 0))
    par = lambda shape: pl.BlockSpec(shape, lambda b, i: (0,) * len(shape))
    return pl.pallas_call(
        _ssd_kernel,
        grid=(bsz, seqlen // t),
        in_specs=[tok(CONV_DIM), tok(DT_PAD), tok(SSD_INNER), par((CONV_WIDTH, CONV_DIM)),
                  par((1, CONV_DIM)), par((DT_PAD, 1)), par((DT_PAD, 1)), par((1, SSD_INNER)),
                  par((1, SSD_INNER))],
        out_specs=tok(SSD_INNER),
        out_shape=jax.ShapeDtypeStruct((bsz, seqlen, SSD_INNER), BF16),
        scratch_shapes=[pltpu.VMEM((t + SUBLANES, CONV_DIM), F32),
                        pltpu.VMEM((SSD_GROUPS, SSD_STATE, HEADS_PER_GROUP * SSD_HEAD_DIM), F32)],
        compiler_params=pltpu.CompilerParams(dimension_semantics=("parallel", "arbitrary"),
                                             vmem_limit_bytes=VMEM_LIMIT),
        name="ssd",
    )(xbc, dt, z, cw, cb, dtb, a, dsk, nw)


def _attn_kernel(q_ref, k0_ref, k1_ref, k2_ref, v0_ref, v1_ref, v2_ref, bias_ref, o_ref):
    lt = pl.program_id(1)
    tq = TQ_ATT
    kpos = lax.broadcasted_iota(jnp.int32, (tq, KV_BLOCKS * tq), 1) + (lt - (KV_BLOCKS - 1)) * tq
    started = kpos >= 0
    scale = ATT_HEAD_DIM ** -0.5
    neg = jnp.finfo(F32).min
    outs = []
    for hd in range(ATT_HEADS):
        d0 = hd * ATT_HEAD_DIM
        qh = q_ref[0, :, d0:d0 + ATT_HEAD_DIM]
        s = jnp.concatenate(
            [lax.dot_general(qh, kr[0, :, d0:d0 + ATT_HEAD_DIM], (((1,), (1,)), ((), ())),
                             preferred_element_type=F32) for kr in (k0_ref, k1_ref, k2_ref)], axis=1)
        s = jnp.where(started, s * scale + bias_ref[hd], neg)
        p = jnp.exp(s - jnp.max(s, axis=-1, keepdims=True))
        denom = jnp.sum(p, axis=-1, keepdims=True)
        pb = p.astype(BF16)
        o = jnp.zeros((tq, ATT_HEAD_DIM), F32)
        for j, vr in enumerate((v0_ref, v1_ref, v2_ref)):
            o = o + jnp.dot(pb[:, j * tq:(j + 1) * tq], vr[0, :, d0:d0 + ATT_HEAD_DIM],
                            preferred_element_type=F32)
        outs.append(o / denom)
    o_ref[0] = jnp.concatenate(outs, axis=1).astype(o_ref.dtype)


def _attn_bias_table(rel_bias):
    tq = TQ_ATT
    nk = KV_BLOCKS * tq
    q0 = (KV_BLOCKS - 1) * tq
    qpos = q0 + np.arange(tq)
    kpos = np.arange(nk)
    dchunk = qpos[:, None] // CHUNK - kpos[None, :] // CHUNK
    band = (dchunk >= 0) & (dchunk <= LEFT_CHUNKS)
    period = nk + tq + 1
    e = np.arange(period)
    e = np.where(e <= nk, e, e - period)
    idx = np.clip(q0 - e, -MAX_REL, MAX_REL) + MAX_REL
    assert np.all(np.diff(idx[q0 - MAX_REL:q0 + MAX_REL + 1]) == -1)
    rb = rel_bias.astype(F32)
    far = rb[:, 2 * MAX_REL:]
    ring = jnp.concatenate([jnp.repeat(far, q0 - MAX_REL, axis=1), rb[:, ::-1],
                            jnp.repeat(far, period - (q0 + MAX_REL + 1), axis=1)], axis=1)
    assert np.all(idx[:q0 - MAX_REL] == 2 * MAX_REL) and np.all(idx[q0 + MAX_REL + 1:] == 2 * MAX_REL)
    tiled = jnp.broadcast_to(ring[:, None, :], (ATT_HEADS, tq, period)).reshape(ATT_HEADS, tq * period)
    bias = tiled[:, :tq * (period - 1)].reshape(ATT_HEADS, tq, period - 1)[:, :, :nk]
    return jnp.where(band[None], bias, jnp.finfo(F32).min)


def _attn(q, k, v, bias):
    bsz, seqlen, _ = q.shape
    tq = TQ_ATT
    blk = lambda back: pl.BlockSpec((1, tq, ATT_INNER), lambda b, i: (b, jnp.maximum(i - back, 0), 0))
    return pl.pallas_call(
        _attn_kernel,
        grid=(bsz, seqlen // tq),
        in_specs=[blk(0), blk(2), blk(1), blk(0), blk(2), blk(1), blk(0),
                  _const_spec((ATT_HEADS, tq, KV_BLOCKS * tq))],
        out_specs=blk(0),
        out_shape=jax.ShapeDtypeStruct((bsz, seqlen, ATT_INNER), BF16),
        compiler_params=pltpu.CompilerParams(dimension_semantics=("parallel", "parallel"),
                                             vmem_limit_bytes=VMEM_LIMIT),
        name="attn",
    )(q, k, k, k, v, v, v, bias)


def _out_ffn2_kernel(x1_ref, y_ref, att_ref, wo_ref, n2_ref, wg_ref, wu_ref, wd_ref, nf_ref,
                     out_ref, act_ref):
    mix = jnp.dot(y_ref[...], wo_ref[0:SSD_INNER, :], preferred_element_type=F32)
    mix = mix + jnp.dot(att_ref[...], wo_ref[SSD_INNER:MIX_WIDTH, :], preferred_element_type=F32)
    x2 = x1_ref[...] + mix
    h = _rms(x2, n2_ref[...]).astype(BF16)
    x3 = x2 + 0.5 * _swiglu(h, wg_ref, wu_ref, wd_ref, act_ref)
    out_ref[...] = _rms(x3, nf_ref[...])


def _out_ffn2(x1, y, att, wo, n2, wg, wu, wd, nf):
    m = x1.shape[0]
    tm = TM_FFN2
    row = lambda n: pl.BlockSpec((tm, n), lambda i: (i, 0))
    return pl.pallas_call(
        _out_ffn2_kernel,
        grid=(m // tm,),
        in_specs=[row(D_MODEL), row(SSD_INNER), row(ATT_INNER), _const_spec((MIX_WIDTH, D_MODEL)),
                  _const_spec((1, D_MODEL)), _const_spec((D_MODEL, D_FF)), _const_spec((D_MODEL, D_FF)),
                  _const_spec((D_FF, D_MODEL)), _const_spec((1, D_MODEL))],
        out_specs=row(D_MODEL),
        out_shape=jax.ShapeDtypeStruct((m, D_MODEL), F32),
        scratch_shapes=[pltpu.VMEM((tm, D_FF), BF16)],
        compiler_params=pltpu.CompilerParams(dimension_semantics=("parallel",),
                                             vmem_limit_bytes=VMEM_LIMIT),
        name="out_ffn2",
    )(x1, y, att, wo, n2, wg, wu, wd, nf)


def _layer(x, ffn1_norm, ffn1_w_gate, ffn1_w_up, ffn1_w_down, mix_norm, w_in, conv_w, conv_b, dt_bias,
           a_log, d_skip, ssd_norm, rel_bias, w_out, ffn2_norm, ffn2_w_gate, ffn2_w_up, ffn2_w_down,
           out_norm):
    bsz, seqlen, _ = x.shape
    m = bsz * seqlen
    o_dt = SSD_INNER + CONV_DIM
    win = jnp.concatenate(
        [w_in[:, :o_dt], w_in[:, o_dt + SSD_HEADS:], w_in[:, o_dt:o_dt + SSD_HEADS],
         jnp.zeros((D_MODEL, DT_PAD - SSD_HEADS), w_in.dtype)], axis=1).astype(BF16)
    row = lambda p: p.reshape(1, -1).astype(F32)
    pad_col = lambda p: jnp.pad(p.astype(F32), (0, DT_PAD - SSD_HEADS)).reshape(DT_PAD, 1)

    x1, z, xbc, q, k, v, dt = _ffn1_proj(
        x.reshape(m, D_MODEL), row(ffn1_norm), ffn1_w_gate.astype(BF16), ffn1_w_up.astype(BF16),
        ffn1_w_down.astype(BF16), row(mix_norm), win)

    seq = lambda arr: arr.reshape(bsz, seqlen, arr.shape[-1])
    y = _ssd(seq(xbc), seq(dt), seq(z), conv_w.astype(F32), row(conv_b), pad_col(dt_bias),
             pad_col(-jnp.exp(a_log.astype(F32))), row(jnp.repeat(d_skip, SSD_HEAD_DIM)), row(ssd_norm))
    att = _attn(seq(q), seq(k), seq(v), _attn_bias_table(rel_bias))

    out = _out_ffn2(x1, y.reshape(m, SSD_INNER), att.reshape(m, ATT_INNER), w_out.astype(BF16),
                    row(ffn2_norm), ffn2_w_gate.astype(BF16), ffn2_w_up.astype(BF16),
                    ffn2_w_down.astype(BF16), row(out_norm))
    return out.reshape(bsz, seqlen, D_MODEL)


def kernel(x, ffn1_norm, ffn1_w_gate, ffn1_w_up, ffn1_w_down, mix_norm, w_in, conv_w, conv_b, dt_bias, a_log, d_skip, ssd_norm, rel_bias, w_out, ffn2_norm, ffn2_w_gate, ffn2_w_up, ffn2_w_down, final_norm):
    depth = ffn1_norm.shape[0]
    assert depth == 1, "the fused out_ffn2 kernel applies the final norm after the only layer"
    return _layer(x, ffn1_norm[0], ffn1_w_gate[0], ffn1_w_up[0], ffn1_w_down[0], mix_norm[0], w_in[0],
                  conv_w[0], conv_b[0], dt_bias[0], a_log[0], d_skip[0], ssd_norm[0], rel_bias[0],
                  w_out[0], ffn2_norm[0], ffn2_w_gate[0], ffn2_w_up[0], ffn2_w_down[0], final_norm)
```

```python
import functools

import jax
import jax.numpy as jnp
import numpy as np
from jax import lax
from jax.experimental import pallas as pl
from jax.experimental.pallas import tpu as pltpu

D_MODEL = 1024
CHUNK = 64
LEFT_CHUNKS = 8
EPS = 1e-5
D_FF = 2816
SSD_HEADS = 16
SSD_HEAD_DIM = 64
SSD_INNER = SSD_HEADS * SSD_HEAD_DIM
SSD_GROUPS = 2
HEADS_PER_GROUP = SSD_HEADS // SSD_GROUPS
SSD_STATE = 128
CONV_WIDTH = 4
CONV_DIM = SSD_INNER + 2 * SSD_GROUPS * SSD_STATE
ATT_HEADS = 8
ATT_HEAD_DIM = 64
ATT_INNER = ATT_HEADS * ATT_HEAD_DIM
MAX_REL = 256
MIX_WIDTH = SSD_INNER + ATT_INNER

LANES = 128
SUBLANES = 8
DT_PAD = LANES
PROJ_PAD = CONV_DIM + SSD_INNER + 2 * ATT_INNER + DT_PAD
BF16_SUBLANES = 16
VT_HEAD_ROWS = ATT_HEAD_DIM + BF16_SUBLANES
VT_ROWS = ATT_HEADS * VT_HEAD_ROWS

TM_FFN1 = 256
TM_FFN2 = 512
T_SSD = 256
TQ_ATT = 256
KV_BLOCKS = 3
F_CHUNKS = ((0, 1024), (1024, 2048), (2048, D_FF))
VMEM_LIMIT = 56 * 1024 * 1024

F32 = jnp.float32
BF16 = jnp.bfloat16


def _rms(x, w):
    return x * lax.rsqrt(jnp.mean(x * x, axis=-1, keepdims=True) + EPS) * w


def _silu(x):
    return x * (1.0 / (1.0 + jnp.exp(-x)))


def _swiglu(h, wg_ref, wu_ref, wd_ref, act_ref):
    for lo, hi in F_CHUNKS:
        g = jnp.dot(h, wg_ref[:, lo:hi], preferred_element_type=F32)
        u = jnp.dot(h, wu_ref[:, lo:hi], preferred_element_type=F32)
        act_ref[:, lo:hi] = (_silu(g) * u).astype(BF16)
    return jnp.dot(act_ref[...], wd_ref[...], preferred_element_type=F32)


def _ffn1_proj_kernel(tiles_per_seq, x_ref, n1_ref, wg_ref, wu_ref, wd_ref, nm_ref, win_ref, wvt_ref,
                      vones_ref, cw_ref, cb_ref, x1_ref, xs_ref, bc_ref, z_ref, q_ref, k_ref, vt_ref,
                      dt_ref, act_ref, buf_ref):
    tm = TM_FFN1
    first_of_seq = pl.program_id(0) % tiles_per_seq == 0

    @pl.when(first_of_seq)
    def _():
        buf_ref[0:SUBLANES, :] = jnp.zeros((SUBLANES, CONV_DIM), F32)

    @pl.when(jnp.logical_not(first_of_seq))
    def _():
        buf_ref[0:SUBLANES, :] = buf_ref[tm:tm + SUBLANES, :]

    x = x_ref[...]
    h = _rms(x, n1_ref[...]).astype(BF16)
    x1 = x + 0.5 * _swiglu(h, wg_ref, wu_ref, wd_ref, act_ref)
    x1_ref[...] = x1
    h2 = _rms(x1, nm_ref[...]).astype(BF16)
    buf_ref[SUBLANES:SUBLANES + tm, :] = jnp.dot(h2, win_ref[:, 0:CONV_DIM], preferred_element_type=F32)

    def conv_silu(c0, c1):
        acc = cb_ref[:, c0:c1]
        for k in range(CONV_WIDTH):
            off = SUBLANES - (CONV_WIDTH - 1) + k
            acc = acc + buf_ref[off:off + tm, c0:c1] * cw_ref[k:k + 1, c0:c1]
        return _silu(acc)

    half = SSD_INNER // 2
    o = CONV_DIM
    vt = lax.dot_general(wvt_ref[...], h2, (((1,), (1,)), ((), ())), preferred_element_type=F32)
    vt_ref[...] = (vt + vones_ref[...]).astype(BF16)
    slabs = ((xs_ref, 0, half), (xs_ref, half, SSD_INNER), (bc_ref, SSD_INNER, CONV_DIM), None)
    for ref, slab in zip((z_ref, q_ref, k_ref, dt_ref), slabs):
        if slab is not None:
            dst, c0, c1 = slab
            base = 0 if dst is xs_ref else SSD_INNER
            dst[:, c0 - base:c1 - base] = conv_silu(c0, c1).astype(dst.dtype)
        n = ref.shape[-1]
        ref[...] = jnp.dot(h2, win_ref[:, o:o + n], preferred_element_type=F32).astype(ref.dtype)
        o += n


def _const_spec(shape):
    return pl.BlockSpec(shape, lambda *_: (0,) * len(shape), pipeline_mode=pl.Buffered(1))


def _ffn1_proj(x, seqlen, n1, wg, wu, wd, nm, win, wvt, vones, cw, cb):
    m = x.shape[0]
    tm = TM_FFN1
    bc_dim = CONV_DIM - SSD_INNER
    row = lambda n: pl.BlockSpec((tm, n), lambda i: (i, 0))
    return pl.pallas_call(
        functools.partial(_ffn1_proj_kernel, seqlen // tm),
        grid=(m // tm,),
        in_specs=[row(D_MODEL), _const_spec((1, D_MODEL)), _const_spec((D_MODEL, D_FF)),
                  _const_spec((D_MODEL, D_FF)), _const_spec((D_FF, D_MODEL)),
                  _const_spec((1, D_MODEL)), _const_spec((D_MODEL, PROJ_PAD)),
                  _const_spec((VT_ROWS, D_MODEL)), _const_spec((VT_ROWS, 1)),
                  _const_spec((CONV_WIDTH, CONV_DIM)), _const_spec((1, CONV_DIM))],
        out_specs=[row(D_MODEL), row(SSD_INNER), row(bc_dim), row(SSD_INNER), row(ATT_INNER),
                   row(ATT_INNER), pl.BlockSpec((VT_ROWS, tm), lambda i: (0, i)), row(DT_PAD)],
        out_shape=[jax.ShapeDtypeStruct((m, D_MODEL), F32),
                   jax.ShapeDtypeStruct((m, SSD_INNER), F32),
                   jax.ShapeDtypeStruct((m, bc_dim), BF16),
                   jax.ShapeDtypeStruct((m, SSD_INNER), F32),
                   jax.ShapeDtypeStruct((m, ATT_INNER), BF16),
                   jax.ShapeDtypeStruct((m, ATT_INNER), BF16),
                   jax.ShapeDtypeStruct((VT_ROWS, m), BF16),
                   jax.ShapeDtypeStruct((m, DT_PAD), F32)],
        scratch_shapes=[pltpu.VMEM((tm, D_FF), BF16), pltpu.VMEM((tm + SUBLANES, CONV_DIM), F32)],
        compiler_params=pltpu.CompilerParams(dimension_semantics=("arbitrary",),
                                             vmem_limit_bytes=VMEM_LIMIT),
        name="ffn1_proj",
    )(x, n1, wg, wu, wd, nm, win, wvt, vones, cw, cb)


def _ssd_kernel(xs_ref, bc_ref, dt_ref, z_ref, dtb_ref, a_ref, dsk_ref, nw_ref, y_ref, h_ref):
    t = T_SSD

    @pl.when(pl.program_id(1) == 0)
    def _():
        h_ref[...] = jnp.zeros_like(h_ref)

    gn = SSD_GROUPS * SSD_STATE
    xs = xs_ref[0]
    bm_bf = bc_ref[0, :, 0:gn]
    cm_bf = bc_ref[0, :, gn:2 * gn]

    dt_raw_t = dt_ref[0].T
    xr = dt_raw_t + dtb_ref[...]
    dt_t = jnp.maximum(xr, 0.0) + jnp.log1p(jnp.exp(-jnp.abs(xr)))
    da_t = dt_t * a_ref[...]
    pos = lax.broadcasted_iota(jnp.int32, (DT_PAD, t), 1)
    acs_t = da_t
    shift = 1
    while shift < t:
        acs_t = acs_t + jnp.where(pos >= shift, pltpu.roll(acs_t, shift, 1), 0.0)
        shift *= 2
    acs_c = acs_t.T
    dt_c = dt_t.T

    causal = (lax.broadcasted_iota(jnp.int32, (t, t), 0) >= lax.broadcasted_iota(jnp.int32, (t, t), 1))
    first = lambda rows: lax.broadcasted_iota(jnp.int32, (rows, 2 * SSD_HEAD_DIM), 1) < SSD_HEAD_DIM
    lo, lo_st, lo_row = first(t), first(SSD_STATE), first(1)
    pw = 2 * SSD_HEAD_DIM
    ys = []
    for g in range(SSD_GROUPS):
        n0 = g * SSD_STATE
        cg = cm_bf[:, n0:n0 + SSD_STATE]
        cbm = lax.dot_general(cg, bm_bf[:, n0:n0 + SSD_STATE], (((1,), (1,)), ((), ())),
                              preferred_element_type=F32)
        bt = bm_bf[:, n0:n0 + SSD_STATE].astype(F32).T
        hprev = h_ref[g]
        yoff = jnp.dot(cg, hprev.astype(BF16), preferred_element_type=F32)
        for pr in range(HEADS_PER_GROUP // 2):
            h0 = g * HEADS_PER_GROUP + 2 * pr
            h1 = h0 + 1
            l0 = pr * pw
            x_pair = xs[:, h0 * SSD_HEAD_DIM:h0 * SSD_HEAD_DIM + pw]
            xdt = (x_pair * jnp.where(lo, dt_c[:, h0:h0 + 1], dt_c[:, h1:h1 + 1])).astype(BF16)
            lhs = []
            for hd in (h0, h1):
                seg = acs_c[:, hd:hd + 1] - acs_t[hd:hd + 1, :]
                lhs.append((cbm * jnp.exp(jnp.where(causal, seg, -jnp.inf))).astype(BF16))
            lasts = [acs_t[hd:hd + 1, t - 1:t] for hd in (h0, h1)]
            for hd, last in zip((h0, h1), lasts):
                lhs.append((bt * jnp.exp(last - acs_t[hd:hd + 1, :])).astype(BF16))
            r = jnp.dot(jnp.concatenate(lhs, axis=0), xdt, preferred_element_type=F32)
            yd = jnp.where(lo, r[0:t], r[t:2 * t])
            st = jnp.where(lo_st, r[2 * t:2 * t + SSD_STATE], r[2 * t + SSD_STATE:])
            ecol = jnp.exp(jnp.where(lo, acs_c[:, h0:h0 + 1], acs_c[:, h1:h1 + 1]))
            ys.append(yd + yoff[:, l0:l0 + pw] * ecol)
            elast = jnp.exp(jnp.where(lo_row, lasts[0], lasts[1]))
            h_ref[g, :, l0:l0 + pw] = hprev[:, l0:l0 + pw] * elast + st

    y = jnp.concatenate(ys, axis=1) + xs * dsk_ref[...]
    y = y * _silu(z_ref[0])
    y_ref[0] = _rms(y, nw_ref[...]).astype(y_ref.dtype)


def _ssd(xs, bc, dt, z, dtb, a, dsk, nw):
    bsz, seqlen, _ = xs.shape
    t = T_SSD
    tok = lambda n: pl.BlockSpec((1, t, n), lambda b, i: (b, i, 0))
    par = lambda shape: pl.BlockSpec(shape, lambda b, i: (0,) * len(shape))
    return pl.pallas_call(
        _ssd_kernel,
        grid=(bsz, seqlen // t),
        in_specs=[tok(SSD_INNER), tok(CONV_DIM - SSD_INNER), tok(DT_PAD), tok(SSD_INNER),
                  par((DT_PAD, 1)), par((DT_PAD, 1)), par((1, SSD_INNER)), par((1, SSD_INNER))],
        out_specs=tok(SSD_INNER),
        out_shape=jax.ShapeDtypeStruct((bsz, seqlen, SSD_INNER), BF16),
        scratch_shapes=[pltpu.VMEM((SSD_GROUPS, SSD_STATE, HEADS_PER_GROUP * SSD_HEAD_DIM), F32)],
        compiler_params=pltpu.CompilerParams(dimension_semantics=("parallel", "arbitrary"),
                                             vmem_limit_bytes=VMEM_LIMIT),
        name="ssd",
    )(xs, bc, dt, z, dtb, a, dsk, nw)


def _attn_kernel(q_ref, k0_ref, k1_ref, k2_ref, vt0_ref, vt1_ref, vt2_ref, bias_ref, o_ref):
    tq = TQ_ATT
    pw = 2 * ATT_HEAD_DIM
    lo = lax.broadcasted_iota(jnp.int32, (tq, pw), 1) < ATT_HEAD_DIM
    nt = (((1,), (1,)), ((), ()))
    for hp in range(ATT_HEADS // 2):
        c0 = hp * pw
        qp = q_ref[0, :, c0:c0 + pw]
        zero = jnp.zeros_like(qp)
        qs = jnp.concatenate([jnp.where(lo, qp, zero), jnp.where(lo, zero, qp)], axis=0)
        st = [lax.dot_general(kr[0, :, c0:c0 + pw], qs, nt, preferred_element_type=F32)
              for kr in (k0_ref, k1_ref, k2_ref)]
        outs = []
        for e in range(2):
            hd = 2 * hp + e
            s = [st[j][:, e * tq:(e + 1) * tq] + bias_ref[0, hd, j * tq:(j + 1) * tq, :]
                 for j in range(KV_BLOCKS)]
            m = jnp.max(jnp.maximum(jnp.maximum(s[0], s[1]), s[2]), axis=0, keepdims=True)
            ot = jnp.zeros((VT_HEAD_ROWS, tq), F32)
            for j, vr in enumerate((vt0_ref, vt1_ref, vt2_ref)):
                p = jnp.exp((s[j] - m).astype(BF16))
                ot = ot + jnp.dot(vr[hd * VT_HEAD_ROWS:(hd + 1) * VT_HEAD_ROWS, :], p,
                                  preferred_element_type=F32)
            outs.append(ot[0:ATT_HEAD_DIM] / ot[ATT_HEAD_DIM:ATT_HEAD_DIM + 1])
        o_ref[0, :, c0:c0 + pw] = jnp.concatenate(outs, axis=0).T.astype(o_ref.dtype)


def _attn_bias_table(rel_bias):
    tq = TQ_ATT
    nk = KV_BLOCKS * tq
    q0 = (KV_BLOCKS - 1) * tq
    qpos = q0 + np.arange(tq)
    kpos = np.arange(nk)
    dchunk = qpos[:, None] // CHUNK - kpos[None, :] // CHUNK
    band = (dchunk >= 0) & (dchunk <= LEFT_CHUNKS)
    period = nk + tq + 1
    e = np.arange(period)
    e = np.where(e <= nk, e, e - period)
    idx = np.clip(q0 - e, -MAX_REL, MAX_REL) + MAX_REL
    assert np.all(np.diff(idx[q0 - MAX_REL:q0 + MAX_REL + 1]) == -1)
    rb = rel_bias.astype(F32)
    far = rb[:, 2 * MAX_REL:]
    ring = jnp.concatenate([jnp.repeat(far, q0 - MAX_REL, axis=1), rb[:, ::-1],
                            jnp.repeat(far, period - (q0 + MAX_REL + 1), axis=1)], axis=1)
    assert np.all(idx[:q0 - MAX_REL] == 2 * MAX_REL) and np.all(idx[q0 + MAX_REL + 1:] == 2 * MAX_REL)
    tiled = jnp.broadcast_to(ring[:, None, :], (ATT_HEADS, tq, period)).reshape(ATT_HEADS, tq * period)
    bias = tiled[:, :tq * (period - 1)].reshape(ATT_HEADS, tq, period - 1)[:, :, :nk]
    started = np.stack([kpos >= (KV_BLOCKS - 1 - n) * tq for n in range(KV_BLOCKS)])
    valid = band[None, None] & started[:, None, None, :]
    return jnp.swapaxes(jnp.where(valid, bias[None], jnp.finfo(F32).min), -1, -2)


def _attn(q, k, vt, bias):
    bsz, seqlen, _ = q.shape
    tq = TQ_ATT
    nt = seqlen // tq
    blk = lambda back: pl.BlockSpec((1, tq, ATT_INNER), lambda b, i: (b, jnp.maximum(i - back, 0), 0))
    vblk = lambda back: pl.BlockSpec((VT_ROWS, tq), lambda b, i: (0, b * nt + jnp.maximum(i - back, 0)))
    return pl.pallas_call(
        _attn_kernel,
        grid=(bsz, nt),
        in_specs=[blk(0), blk(2), blk(1), blk(0), vblk(2), vblk(1), vblk(0),
                  pl.BlockSpec((1, ATT_HEADS, KV_BLOCKS * tq, tq),
                               lambda b, i: (jnp.minimum(i, KV_BLOCKS - 1), 0, 0, 0))],
        out_specs=blk(0),
        out_shape=jax.ShapeDtypeStruct((bsz, seqlen, ATT_INNER), BF16),
        compiler_params=pltpu.CompilerParams(dimension_semantics=("parallel", "parallel"),
                                             vmem_limit_bytes=VMEM_LIMIT),
        name="attn",
    )(q, k, k, k, vt, vt, vt, bias)


def _out_ffn2_kernel(x1_ref, y_ref, att_ref, wo_ref, n2_ref, wg_ref, wu_ref, wd_ref, nf_ref,
                     out_ref, act_ref):
    mix = jnp.dot(y_ref[...], wo_ref[0:SSD_INNER, :], preferred_element_type=F32)
    mix = mix + jnp.dot(att_ref[...], wo_ref[SSD_INNER:MIX_WIDTH, :], preferred_element_type=F32)
    x2 = x1_ref[...] + mix
    h = _rms(x2, n2_ref[...]).astype(BF16)
    x3 = x2 + 0.5 * _swiglu(h, wg_ref, wu_ref, wd_ref, act_ref)
    out_ref[...] = _rms(x3, nf_ref[...])


def _out_ffn2(x1, y, att, wo, n2, wg, wu, wd, nf):
    m = x1.shape[0]
    tm = TM_FFN2
    row = lambda n: pl.BlockSpec((tm, n), lambda i: (i, 0))
    return pl.pallas_call(
        _out_ffn2_kernel,
        grid=(m // tm,),
        in_specs=[row(D_MODEL), row(SSD_INNER), row(ATT_INNER), _const_spec((MIX_WIDTH, D_MODEL)),
                  _const_spec((1, D_MODEL)), _const_spec((D_MODEL, D_FF)), _const_spec((D_MODEL, D_FF)),
                  _const_spec((D_FF, D_MODEL)), _const_spec((1, D_MODEL))],
        out_specs=row(D_MODEL),
        out_shape=jax.ShapeDtypeStruct((m, D_MODEL), F32),
        scratch_shapes=[pltpu.VMEM((tm, D_FF), BF16)],
        compiler_params=pltpu.CompilerParams(dimension_semantics=("parallel",),
                                             vmem_limit_bytes=VMEM_LIMIT),
        name="out_ffn2",
    )(x1, y, att, wo, n2, wg, wu, wd, nf)


def _layer(x, ffn1_norm, ffn1_w_gate, ffn1_w_up, ffn1_w_down, mix_norm, w_in, conv_w, conv_b, dt_bias,
           a_log, d_skip, ssd_norm, rel_bias, w_out, ffn2_norm, ffn2_w_gate, ffn2_w_up, ffn2_w_down,
           out_norm):
    bsz, seqlen, _ = x.shape
    m = bsz * seqlen
    o_xbc = SSD_INNER
    o_dt = o_xbc + CONV_DIM
    o_q = o_dt + SSD_HEADS
    o_k = o_q + ATT_INNER
    o_v = o_k + ATT_INNER
    win = jnp.concatenate(
        [w_in[:, o_xbc:o_dt], w_in[:, :o_xbc], w_in[:, o_q:o_k] * ATT_HEAD_DIM ** -0.5, w_in[:, o_k:o_v],
         w_in[:, o_dt:o_q], jnp.zeros((D_MODEL, DT_PAD - SSD_HEADS), w_in.dtype)], axis=1).astype(BF16)
    pad_rows = ((0, 0), (0, VT_HEAD_ROWS - ATT_HEAD_DIM), (0, 0))
    wvt = jnp.pad(w_in[:, o_v:].T.reshape(ATT_HEADS, ATT_HEAD_DIM, D_MODEL), pad_rows)
    wvt = wvt.reshape(VT_ROWS, D_MODEL).astype(BF16)
    vones = jnp.zeros((ATT_HEADS, VT_HEAD_ROWS), F32).at[:, ATT_HEAD_DIM].set(1.0).reshape(VT_ROWS, 1)
    row = lambda p: p.reshape(1, -1).astype(F32)
    pad_col = lambda p: jnp.pad(p.astype(F32), (0, DT_PAD - SSD_HEADS)).reshape(DT_PAD, 1)

    x1, xs, bc, z, q, k, vt, dt = _ffn1_proj(
        x.reshape(m, D_MODEL), seqlen, row(ffn1_norm), ffn1_w_gate.astype(BF16), ffn1_w_up.astype(BF16),
        ffn1_w_down.astype(BF16), row(mix_norm), win, wvt, vones, conv_w.astype(F32), row(conv_b))

    seq = lambda arr: arr.reshape(bsz, seqlen, arr.shape[-1])
    y = _ssd(seq(xs), seq(bc), seq(dt), seq(z), pad_col(dt_bias), pad_col(-jnp.exp(a_log.astype(F32))),
             row(jnp.repeat(d_skip, SSD_HEAD_DIM)), row(ssd_norm))
    att = _attn(seq(q), seq(k), vt, _attn_bias_table(rel_bias))

    out = _out_ffn2(x1, y.reshape(m, SSD_INNER), att.reshape(m, ATT_INNER), w_out.astype(BF16),
                    row(ffn2_norm), ffn2_w_gate.astype(BF16), ffn2_w_up.astype(BF16),
                    ffn2_w_down.astype(BF16), row(out_norm))
    return out.reshape(bsz, seqlen, D_MODEL)


def kernel(x, ffn1_norm, ffn1_w_gate, ffn1_w_up, ffn1_w_down, mix_norm, w_in, conv_w, conv_b, dt_bias, a_log, d_skip, ssd_norm, rel_bias, w_out, ffn2_norm, ffn2_w_gate, ffn2_w_up, ffn2_w_down, final_norm):
    depth = ffn1_norm.shape[0]
    assert depth == 1, "the fused out_ffn2 kernel applies the final norm after the only layer"
    return _layer(x, ffn1_norm[0], ffn1_w_gate[0], ffn1_w_up[0], ffn1_w_down[0], mix_norm[0], w_in[0],
                  conv_w[0], conv_b[0], dt_bias[0], a_log[0], d_skip[0], ssd_norm[0], rel_bias[0],
                  w_out[0], ffn2_norm[0], ffn2_w_gate[0], ffn2_w_up[0], ffn2_w_down[0], final_norm)
```

```python
import functools
import math

import jax
import jax.numpy as jnp
import numpy as np
from jax import lax
from jax.experimental import pallas as pl
from jax.experimental.pallas import tpu as pltpu

D_MODEL = 1024
CHUNK = 64
LEFT_CHUNKS = 8
EPS = 1e-5
D_FF = 2816
SSD_HEADS = 16
SSD_HEAD_DIM = 64
SSD_INNER = SSD_HEADS * SSD_HEAD_DIM
SSD_GROUPS = 2
HEADS_PER_GROUP = SSD_HEADS // SSD_GROUPS
SSD_STATE = 128
CONV_WIDTH = 4
CONV_DIM = SSD_INNER + 2 * SSD_GROUPS * SSD_STATE
ATT_HEADS = 8
ATT_HEAD_DIM = 64
ATT_INNER = ATT_HEADS * ATT_HEAD_DIM
MAX_REL = 256
MIX_WIDTH = SSD_INNER + ATT_INNER

LANES = 128
SUBLANES = 8
DT_PAD = LANES
PROJ_PAD = CONV_DIM + SSD_INNER + 2 * ATT_INNER + DT_PAD
BF16_SUBLANES = 16
VT_HEAD_ROWS = ATT_HEAD_DIM + BF16_SUBLANES
VT_ROWS = ATT_HEADS * VT_HEAD_ROWS

TM_FFN1 = 256
TM_FFN2 = 512
T_SSD = 256
TQ_ATT = 256
KV_BLOCKS = 3
F_CHUNKS = ((0, 1024), (1024, 2048), (2048, D_FF))
VMEM_LIMIT = 56 * 1024 * 1024

F32 = jnp.float32
BF16 = jnp.bfloat16


def _rms(x, w):
    return x * lax.rsqrt(jnp.mean(x * x, axis=-1, keepdims=True) + EPS) * w


def _silu(x):
    return x * (1.0 / (1.0 + jnp.exp(-x)))


def _swiglu(h, wg_ref, wu_ref, wd_ref, act_ref):
    for lo, hi in F_CHUNKS:
        g = jnp.dot(h, wg_ref[:, lo:hi], preferred_element_type=F32)
        u = jnp.dot(h, wu_ref[:, lo:hi], preferred_element_type=F32)
        act_ref[:, lo:hi] = (_silu(g) * u).astype(BF16)
    return jnp.dot(act_ref[...], wd_ref[...], preferred_element_type=F32)


def _ffn1_proj_kernel(tiles_per_seq, x_ref, n1_ref, wg_ref, wu_ref, wd_ref, nm_ref, win_ref, wvt_ref,
                      vones_ref, cw_ref, cb_ref, dtb_ref, a_ref, x1_ref, xs_ref, bc_ref, z_ref, q_ref,
                      k_ref, vt_ref, dtc_ref, acsc_ref, acst_ref, act_ref, buf_ref):
    tm = TM_FFN1
    first_of_seq = pl.program_id(0) % tiles_per_seq == 0

    @pl.when(first_of_seq)
    def _():
        buf_ref[0:SUBLANES, :] = jnp.zeros((SUBLANES, CONV_DIM), F32)

    @pl.when(jnp.logical_not(first_of_seq))
    def _():
        buf_ref[0:SUBLANES, :] = buf_ref[tm:tm + SUBLANES, :]

    x = x_ref[...]
    h = _rms(x, n1_ref[...]).astype(BF16)
    x1 = x + 0.5 * _swiglu(h, wg_ref, wu_ref, wd_ref, act_ref)
    x1_ref[...] = x1
    h2 = _rms(x1, nm_ref[...]).astype(BF16)

    o = PROJ_PAD - DT_PAD
    dtc_ref[...] = jnp.dot(h2, win_ref[:, o:o + DT_PAD], preferred_element_type=F32)
    xr = dtc_ref[...].T + dtb_ref[...]
    dt_t = jnp.maximum(xr, 0.0) + jnp.log1p(jnp.exp(-jnp.abs(xr)))
    acs_t = dt_t * a_ref[...]
    pos = lax.broadcasted_iota(jnp.int32, (DT_PAD, tm), 1)
    shift = 1
    while shift < tm:
        acs_t = acs_t + jnp.where(pos >= shift, pltpu.roll(acs_t, shift, 1), 0.0)
        shift *= 2
    acst_ref[...] = acs_t
    dtc_ref[...] = dt_t.T
    acsc_ref[...] = acs_t.T

    buf_ref[SUBLANES:SUBLANES + tm, :] = jnp.dot(h2, win_ref[:, 0:CONV_DIM], preferred_element_type=F32)
    vt = lax.dot_general(wvt_ref[...], h2, (((1,), (1,)), ((), ())), preferred_element_type=F32)
    vt_ref[...] = (vt + vones_ref[...]).astype(BF16)
    o = CONV_DIM
    for ref in (z_ref, q_ref, k_ref):
        n = ref.shape[-1]
        ref[...] = jnp.dot(h2, win_ref[:, o:o + n], preferred_element_type=F32).astype(ref.dtype)
        o += n

    acc = cb_ref[...]
    for k in range(CONV_WIDTH):
        off = SUBLANES - (CONV_WIDTH - 1) + k
        acc = acc + buf_ref[off:off + tm, :] * cw_ref[k:k + 1, :]
    xbc = _silu(acc)
    xs_ref[...] = xbc[:, :SSD_INNER]
    bc_ref[...] = xbc[:, SSD_INNER:].astype(BF16)


def _const_spec(shape):
    return pl.BlockSpec(shape, lambda *_: (0,) * len(shape), pipeline_mode=pl.Buffered(1))


def _ffn1_proj(x, seqlen, n1, wg, wu, wd, nm, win, wvt, vones, cw, cb, dtb, a):
    m = x.shape[0]
    tm = TM_FFN1
    bc_dim = CONV_DIM - SSD_INNER
    row = lambda n: pl.BlockSpec((tm, n), lambda i: (i, 0))
    col = lambda n: pl.BlockSpec((n, tm), lambda i: (0, i))
    return pl.pallas_call(
        functools.partial(_ffn1_proj_kernel, seqlen // tm),
        grid=(m // tm,),
        in_specs=[row(D_MODEL), _const_spec((1, D_MODEL)), _const_spec((D_MODEL, D_FF)),
                  _const_spec((D_MODEL, D_FF)), _const_spec((D_FF, D_MODEL)),
                  _const_spec((1, D_MODEL)), _const_spec((D_MODEL, PROJ_PAD)),
                  _const_spec((VT_ROWS, D_MODEL)), _const_spec((VT_ROWS, 1)),
                  _const_spec((CONV_WIDTH, CONV_DIM)), _const_spec((1, CONV_DIM)),
                  _const_spec((DT_PAD, 1)), _const_spec((DT_PAD, 1))],
        out_specs=[row(D_MODEL), row(SSD_INNER), row(bc_dim), row(SSD_INNER), row(ATT_INNER),
                   row(ATT_INNER), col(VT_ROWS), row(DT_PAD), row(DT_PAD), col(DT_PAD)],
        out_shape=[jax.ShapeDtypeStruct((m, D_MODEL), F32),
                   jax.ShapeDtypeStruct((m, SSD_INNER), F32),
                   jax.ShapeDtypeStruct((m, bc_dim), BF16),
                   jax.ShapeDtypeStruct((m, SSD_INNER), F32),
                   jax.ShapeDtypeStruct((m, ATT_INNER), BF16),
                   jax.ShapeDtypeStruct((m, ATT_INNER), BF16),
                   jax.ShapeDtypeStruct((VT_ROWS, m), BF16),
                   jax.ShapeDtypeStruct((m, DT_PAD), F32),
                   jax.ShapeDtypeStruct((m, DT_PAD), F32),
                   jax.ShapeDtypeStruct((DT_PAD, m), F32)],
        scratch_shapes=[pltpu.VMEM((tm, D_FF), BF16), pltpu.VMEM((tm + SUBLANES, CONV_DIM), F32)],
        compiler_params=pltpu.CompilerParams(dimension_semantics=("arbitrary",),
                                             vmem_limit_bytes=VMEM_LIMIT),
        name="ffn1_proj",
    )(x, n1, wg, wu, wd, nm, win, wvt, vones, cw, cb, dtb, a)


def _ssd_kernel(xs_ref, bc_ref, dtc_ref, acsc_ref, acst_ref, z_ref, dsk_ref, nw_ref, y_ref, h_ref):
    t = T_SSD
    hb = t // 2

    @pl.when(pl.program_id(1) == 0)
    def _():
        h_ref[...] = jnp.zeros_like(h_ref)

    gn = SSD_GROUPS * SSD_STATE
    xs = xs_ref[0]
    bm_bf = bc_ref[0, :, 0:gn]
    cm_bf = bc_ref[0, :, gn:2 * gn]
    dt_c = dtc_ref[0]
    acs_c = acsc_ref[0]
    acs_t = acst_ref[...]

    causal = (lax.broadcasted_iota(jnp.int32, (hb, hb), 0) >= lax.broadcasted_iota(jnp.int32, (hb, hb), 1))
    first = lambda rows: lax.broadcasted_iota(jnp.int32, (rows, 2 * SSD_HEAD_DIM), 1) < SSD_HEAD_DIM
    lo, lo_st, lo_row = first(t), first(SSD_STATE), first(1)
    pw = 2 * SSD_HEAD_DIM
    ys = []
    for g in range(SSD_GROUPS):
        n0 = g * SSD_STATE
        cg = cm_bf[:, n0:n0 + SSD_STATE]
        cbm = lax.dot_general(cg, bm_bf[:, n0:n0 + SSD_STATE], (((1,), (1,)), ((), ())),
                              preferred_element_type=F32)
        bt = bm_bf[:, n0:n0 + SSD_STATE].astype(F32).T
        hprev = h_ref[g]
        yoff = jnp.dot(cg, hprev.astype(BF16), preferred_element_type=F32)
        for pr in range(HEADS_PER_GROUP // 2):
            h0 = g * HEADS_PER_GROUP + 2 * pr
            h1 = h0 + 1
            l0 = pr * pw
            x_pair = xs[:, h0 * SSD_HEAD_DIM:h0 * SSD_HEAD_DIM + pw]
            xdt = (x_pair * jnp.where(lo, dt_c[:, h0:h0 + 1], dt_c[:, h1:h1 + 1])).astype(BF16)
            lhs = []
            for hd in (h0, h1):
                col = acs_c[:, hd:hd + 1]
                row = acs_t[hd:hd + 1, :]
                d00 = jnp.exp2(jnp.where(causal, col[:hb] - row[:, :hb], -jnp.inf))
                d10 = jnp.exp2(col[hb:] - row[:, :hb])
                d11 = jnp.exp2(jnp.where(causal, col[hb:] - row[:, hb:], -jnp.inf))
                top = jnp.concatenate([(cbm[:hb, :hb] * d00).astype(BF16), jnp.zeros((hb, hb), BF16)], axis=1)
                bot = jnp.concatenate([(cbm[hb:, :hb] * d10).astype(BF16),
                                       (cbm[hb:, hb:] * d11).astype(BF16)], axis=1)
                lhs += [top, bot]
            lasts = [acs_t[hd:hd + 1, t - 1:t] for hd in (h0, h1)]
            for hd, last in zip((h0, h1), lasts):
                lhs.append((bt * jnp.exp2(last - acs_t[hd:hd + 1, :])).astype(BF16))
            r = jnp.dot(jnp.concatenate(lhs, axis=0), xdt, preferred_element_type=F32)
            yd = jnp.where(lo, r[0:t], r[t:2 * t])
            st = jnp.where(lo_st, r[2 * t:2 * t + SSD_STATE], r[2 * t + SSD_STATE:])
            ecol = jnp.exp2(jnp.where(lo, acs_c[:, h0:h0 + 1], acs_c[:, h1:h1 + 1]))
            ys.append(yd + yoff[:, l0:l0 + pw] * ecol)
            elast = jnp.exp2(jnp.where(lo_row, lasts[0], lasts[1]))
            h_ref[g, :, l0:l0 + pw] = hprev[:, l0:l0 + pw] * elast + st

    y = jnp.concatenate(ys, axis=1) + xs * dsk_ref[...]
    y = y * _silu(z_ref[0])
    y_ref[0] = _rms(y, nw_ref[...]).astype(y_ref.dtype)


def _ssd(xs, bc, dtc, acsc, acst, z, dsk, nw):
    bsz, seqlen, _ = xs.shape
    t = T_SSD
    nt = seqlen // t
    tok = lambda n: pl.BlockSpec((1, t, n), lambda b, i: (b, i, 0))
    par = lambda shape: pl.BlockSpec(shape, lambda b, i: (0,) * len(shape))
    return pl.pallas_call(
        _ssd_kernel,
        grid=(bsz, nt),
        in_specs=[tok(SSD_INNER), tok(CONV_DIM - SSD_INNER), tok(DT_PAD), tok(DT_PAD),
                  pl.BlockSpec((DT_PAD, t), lambda b, i: (0, b * nt + i)), tok(SSD_INNER),
                  par((1, SSD_INNER)), par((1, SSD_INNER))],
        out_specs=tok(SSD_INNER),
        out_shape=jax.ShapeDtypeStruct((bsz, seqlen, SSD_INNER), BF16),
        scratch_shapes=[pltpu.VMEM((SSD_GROUPS, SSD_STATE, HEADS_PER_GROUP * SSD_HEAD_DIM), F32)],
        compiler_params=pltpu.CompilerParams(dimension_semantics=("parallel", "arbitrary"),
                                             vmem_limit_bytes=VMEM_LIMIT),
        name="ssd",
    )(xs, bc, dtc, acsc, acst, z, dsk, nw)


def _attn_kernel(q_ref, k0_ref, k1_ref, k2_ref, vt0_ref, vt1_ref, vt2_ref, bias_ref, o_ref, st_ref):
    tq = TQ_ATT
    pw = 2 * ATT_HEAD_DIM
    lo = lax.broadcasted_iota(jnp.int32, (tq, pw), 1) < ATT_HEAD_DIM
    nt = (((1,), (1,)), ((), ()))
    for hp in range(ATT_HEADS // 2):
        c0 = hp * pw
        qp = q_ref[0, :, c0:c0 + pw]
        zero = jnp.zeros_like(qp)
        qs = jnp.concatenate([jnp.where(lo, qp, zero), jnp.where(lo, zero, qp)], axis=0)
        for j, kr in enumerate((k0_ref, k1_ref, k2_ref)):
            st_ref[hp, j] = lax.dot_general(kr[0, :, c0:c0 + pw], qs, nt, preferred_element_type=F32)
    for hp in range(ATT_HEADS // 2):
        c0 = hp * pw
        outs = []
        for e in range(2):
            hd = 2 * hp + e
            s = [st_ref[hp, j, :, e * tq:(e + 1) * tq] + bias_ref[0, hd, j * tq:(j + 1) * tq, :]
                 for j in range(KV_BLOCKS)]
            m = jnp.max(jnp.maximum(jnp.maximum(s[0], s[1]), s[2]), axis=0, keepdims=True)
            ot = jnp.zeros((VT_HEAD_ROWS, tq), F32)
            for j, vr in enumerate((vt0_ref, vt1_ref, vt2_ref)):
                p = jnp.exp((s[j] - m).astype(BF16))
                ot = ot + jnp.dot(vr[hd * VT_HEAD_ROWS:(hd + 1) * VT_HEAD_ROWS, :], p,
                                  preferred_element_type=F32)
            outs.append(ot[0:ATT_HEAD_DIM] / ot[ATT_HEAD_DIM:ATT_HEAD_DIM + 1])
        o_ref[0, :, c0:c0 + pw] = jnp.concatenate(outs, axis=0).T.astype(o_ref.dtype)


def _attn_bias_table(rel_bias):
    tq = TQ_ATT
    nk = KV_BLOCKS * tq
    q0 = (KV_BLOCKS - 1) * tq
    qpos = q0 + np.arange(tq)
    kpos = np.arange(nk)
    dchunk = qpos[:, None] // CHUNK - kpos[None, :] // CHUNK
    band = (dchunk >= 0) & (dchunk <= LEFT_CHUNKS)
    period = nk + tq + 1
    e = np.arange(period)
    e = np.where(e <= nk, e, e - period)
    idx = np.clip(q0 - e, -MAX_REL, MAX_REL) + MAX_REL
    assert np.all(np.diff(idx[q0 - MAX_REL:q0 + MAX_REL + 1]) == -1)
    rb = rel_bias.astype(F32)
    far = rb[:, 2 * MAX_REL:]
    ring = jnp.concatenate([jnp.repeat(far, q0 - MAX_REL, axis=1), rb[:, ::-1],
                            jnp.repeat(far, period - (q0 + MAX_REL + 1), axis=1)], axis=1)
    assert np.all(idx[:q0 - MAX_REL] == 2 * MAX_REL) and np.all(idx[q0 + MAX_REL + 1:] == 2 * MAX_REL)
    tiled = jnp.broadcast_to(ring[:, None, :], (ATT_HEADS, tq, period)).reshape(ATT_HEADS, tq * period)
    bias = tiled[:, :tq * (period - 1)].reshape(ATT_HEADS, tq, period - 1)[:, :, :nk]
    started = np.stack([kpos >= (KV_BLOCKS - 1 - n) * tq for n in range(KV_BLOCKS)])
    valid = band[None, None] & started[:, None, None, :]
    return jnp.swapaxes(jnp.where(valid, bias[None], jnp.finfo(F32).min), -1, -2)


def _attn(q, k, vt, bias):
    bsz, seqlen, _ = q.shape
    tq = TQ_ATT
    nt = seqlen // tq
    blk = lambda back: pl.BlockSpec((1, tq, ATT_INNER), lambda b, i: (b, jnp.maximum(i - back, 0), 0))
    vblk = lambda back: pl.BlockSpec((VT_ROWS, tq), lambda b, i: (0, b * nt + jnp.maximum(i - back, 0)))
    return pl.pallas_call(
        _attn_kernel,
        grid=(bsz, nt),
        in_specs=[blk(0), blk(2), blk(1), blk(0), vblk(2), vblk(1), vblk(0),
                  pl.BlockSpec((1, ATT_HEADS, KV_BLOCKS * tq, tq),
                               lambda b, i: (jnp.minimum(i, KV_BLOCKS - 1), 0, 0, 0))],
        out_specs=blk(0),
        out_shape=jax.ShapeDtypeStruct((bsz, seqlen, ATT_INNER), BF16),
        scratch_shapes=[pltpu.VMEM((ATT_HEADS // 2, KV_BLOCKS, tq, 2 * tq), F32)],
        compiler_params=pltpu.CompilerParams(dimension_semantics=("parallel", "parallel"),
                                             vmem_limit_bytes=VMEM_LIMIT),
        name="attn",
    )(q, k, k, k, vt, vt, vt, bias)


def _out_ffn2_kernel(x1_ref, y_ref, att_ref, wo_ref, n2_ref, wg_ref, wu_ref, wd_ref, nf_ref,
                     out_ref, act_ref):
    mix = jnp.dot(y_ref[...], wo_ref[0:SSD_INNER, :], preferred_element_type=F32)
    mix = mix + jnp.dot(att_ref[...], wo_ref[SSD_INNER:MIX_WIDTH, :], preferred_element_type=F32)
    x2 = x1_ref[...] + mix
    h = _rms(x2, n2_ref[...]).astype(BF16)
    x3 = x2 + 0.5 * _swiglu(h, wg_ref, wu_ref, wd_ref, act_ref)
    out_ref[...] = _rms(x3, nf_ref[...])


def _out_ffn2(x1, y, att, wo, n2, wg, wu, wd, nf):
    m = x1.shape[0]
    tm = TM_FFN2
    row = lambda n: pl.BlockSpec((tm, n), lambda i: (i, 0))
    return pl.pallas_call(
        _out_ffn2_kernel,
        grid=(m // tm,),
        in_specs=[row(D_MODEL), row(SSD_INNER), row(ATT_INNER), _const_spec((MIX_WIDTH, D_MODEL)),
                  _const_spec((1, D_MODEL)), _const_spec((D_MODEL, D_FF)), _const_spec((D_MODEL, D_FF)),
                  _const_spec((D_FF, D_MODEL)), _const_spec((1, D_MODEL))],
        out_specs=row(D_MODEL),
        out_shape=jax.ShapeDtypeStruct((m, D_MODEL), F32),
        scratch_shapes=[pltpu.VMEM((tm, D_FF), BF16)],
        compiler_params=pltpu.CompilerParams(dimension_semantics=("parallel",),
                                             vmem_limit_bytes=VMEM_LIMIT),
        name="out_ffn2",
    )(x1, y, att, wo, n2, wg, wu, wd, nf)


def _layer(x, ffn1_norm, ffn1_w_gate, ffn1_w_up, ffn1_w_down, mix_norm, w_in, conv_w, conv_b, dt_bias,
           a_log, d_skip, ssd_norm, rel_bias, w_out, ffn2_norm, ffn2_w_gate, ffn2_w_up, ffn2_w_down,
           out_norm):
    bsz, seqlen, _ = x.shape
    m = bsz * seqlen
    o_xbc = SSD_INNER
    o_dt = o_xbc + CONV_DIM
    o_q = o_dt + SSD_HEADS
    o_k = o_q + ATT_INNER
    o_v = o_k + ATT_INNER
    win = jnp.concatenate(
        [w_in[:, o_xbc:o_dt], w_in[:, :o_xbc], w_in[:, o_q:o_k] * ATT_HEAD_DIM ** -0.5, w_in[:, o_k:o_v],
         w_in[:, o_dt:o_q], jnp.zeros((D_MODEL, DT_PAD - SSD_HEADS), w_in.dtype)], axis=1).astype(BF16)
    pad_rows = ((0, 0), (0, VT_HEAD_ROWS - ATT_HEAD_DIM), (0, 0))
    wvt = jnp.pad(w_in[:, o_v:].T.reshape(ATT_HEADS, ATT_HEAD_DIM, D_MODEL), pad_rows)
    wvt = wvt.reshape(VT_ROWS, D_MODEL).astype(BF16)
    vones = jnp.zeros((ATT_HEADS, VT_HEAD_ROWS), F32).at[:, ATT_HEAD_DIM].set(1.0).reshape(VT_ROWS, 1)
    row = lambda p: p.reshape(1, -1).astype(F32)
    pad_col = lambda p: jnp.pad(p.astype(F32), (0, DT_PAD - SSD_HEADS)).reshape(DT_PAD, 1)

    a_log2 = -jnp.exp(a_log.astype(F32)) * math.log2(math.e)
    x1, xs, bc, z, q, k, vt, dtc, acsc, acst = _ffn1_proj(
        x.reshape(m, D_MODEL), seqlen, row(ffn1_norm), ffn1_w_gate.astype(BF16), ffn1_w_up.astype(BF16),
        ffn1_w_down.astype(BF16), row(mix_norm), win, wvt, vones, conv_w.astype(F32), row(conv_b),
        pad_col(dt_bias), pad_col(a_log2))

    seq = lambda arr: arr.reshape(bsz, seqlen, arr.shape[-1])
    y = _ssd(seq(xs), seq(bc), seq(dtc), seq(acsc), acst, seq(z),
             row(jnp.repeat(d_skip, SSD_HEAD_DIM)), row(ssd_norm))
    att = _attn(seq(q), seq(k), vt, _attn_bias_table(rel_bias))

    out = _out_ffn2(x1, y.reshape(m, SSD_INNER), att.reshape(m, ATT_INNER), w_out.astype(BF16),
                    row(ffn2_norm), ffn2_w_gate.astype(BF16), ffn2_w_up.astype(BF16),
                    ffn2_w_down.astype(BF16), row(out_norm))
    return out.reshape(bsz, seqlen, D_MODEL)


def kernel(x, ffn1_norm, ffn1_w_gate, ffn1_w_up, ffn1_w_down, mix_norm, w_in, conv_w, conv_b, dt_bias, a_log, d_skip, ssd_norm, rel_bias, w_out, ffn2_norm, ffn2_w_gate, ffn2_w_up, ffn2_w_down, final_norm):
    depth = ffn1_norm.shape[0]
    assert depth == 1, "the fused out_ffn2 kernel applies the final norm after the only layer"
    return _layer(x, ffn1_norm[0], ffn1_w_gate[0], ffn1_w_up[0], ffn1_w_down[0], mix_norm[0], w_in[0],
                  conv_w[0], conv_b[0], dt_bias[0], a_log[0], d_skip[0], ssd_norm[0], rel_bias[0],
                  w_out[0], ffn2_norm[0], ffn2_w_gate[0], ffn2_w_up[0], ffn2_w_down[0], final_norm)
```

```python
import functools
import math

import jax
import jax.numpy as jnp
import numpy as np
from jax import lax
from jax.experimental import pallas as pl
from jax.experimental.pallas import tpu as pltpu

D_MODEL = 1024
CHUNK = 64
LEFT_CHUNKS = 8
EPS = 1e-5
D_FF = 2816
SSD_HEADS = 16
SSD_HEAD_DIM = 64
SSD_INNER = SSD_HEADS * SSD_HEAD_DIM
SSD_GROUPS = 2
HEADS_PER_GROUP = SSD_HEADS // SSD_GROUPS
SSD_STATE = 128
CONV_WIDTH = 4
CONV_DIM = SSD_INNER + 2 * SSD_GROUPS * SSD_STATE
ATT_HEADS = 8
ATT_HEAD_DIM = 64
ATT_INNER = ATT_HEADS * ATT_HEAD_DIM
MAX_REL = 256
MIX_WIDTH = SSD_INNER + ATT_INNER

LANES = 128
SUBLANES = 8
DT_PAD = LANES
PROJ_PAD = CONV_DIM + SSD_INNER + 2 * ATT_INNER + DT_PAD
BF16_SUBLANES = 16
VT_HEAD_ROWS = ATT_HEAD_DIM + BF16_SUBLANES
VT_ROWS = ATT_HEADS * VT_HEAD_ROWS

TM_FFN1 = 256
TM_FFN2 = 512
T_SSD = 256
TQ_ATT = 256
KV_BLOCKS = 3
ATT_RING = (KV_BLOCKS + 1) * TQ_ATT
F_CHUNKS = ((0, 1024), (1024, 2048), (2048, D_FF))
VMEM_LIMIT = 56 * 1024 * 1024

F32 = jnp.float32
BF16 = jnp.bfloat16


def _rms(x, w):
    return x * lax.rsqrt(jnp.mean(x * x, axis=-1, keepdims=True) + EPS) * w


def _silu(x):
    return x * (1.0 / (1.0 + jnp.exp(-x)))


def _swiglu(h, wg_ref, wu_ref, wd_ref, act_ref):
    for lo, hi in F_CHUNKS:
        g = jnp.dot(h, wg_ref[:, lo:hi], preferred_element_type=F32)
        u = jnp.dot(h, wu_ref[:, lo:hi], preferred_element_type=F32)
        act_ref[:, lo:hi] = (_silu(g) * u).astype(BF16)
    return jnp.dot(act_ref[...], wd_ref[...], preferred_element_type=F32)


def _ffn1_proj_kernel(tiles_per_seq, x_ref, n1_ref, wg_ref, wu_ref, wd_ref, nm_ref, win_ref, wvt_ref,
                      vones_ref, cw_ref, cb_ref, dtb_ref, a_ref, x1_ref, xs_ref, bc_ref, z_ref, q_ref,
                      k_ref, vt_ref, dtc_ref, acsc_ref, acst_ref, act_ref, buf_ref):
    tm = TM_FFN1
    first_of_seq = pl.program_id(0) % tiles_per_seq == 0

    @pl.when(first_of_seq)
    def _():
        buf_ref[0:SUBLANES, :] = jnp.zeros((SUBLANES, CONV_DIM), F32)

    @pl.when(jnp.logical_not(first_of_seq))
    def _():
        buf_ref[0:SUBLANES, :] = buf_ref[tm:tm + SUBLANES, :]

    x = x_ref[...]
    h = _rms(x, n1_ref[...]).astype(BF16)
    x1 = x + 0.5 * _swiglu(h, wg_ref, wu_ref, wd_ref, act_ref)
    x1_ref[...] = x1
    h2 = _rms(x1, nm_ref[...]).astype(BF16)

    o = PROJ_PAD - DT_PAD
    dtc_ref[...] = jnp.dot(h2, win_ref[:, o:o + DT_PAD], preferred_element_type=F32)
    xr = dtc_ref[...].T + dtb_ref[...]
    dt_t = jnp.maximum(xr, 0.0) + jnp.log1p(jnp.exp(-jnp.abs(xr)))
    acs_t = dt_t * a_ref[...]
    pos = lax.broadcasted_iota(jnp.int32, (DT_PAD, tm), 1)
    shift = 1
    while shift < tm:
        acs_t = acs_t + jnp.where(pos >= shift, pltpu.roll(acs_t, shift, 1), 0.0)
        shift *= 2
    acst_ref[...] = acs_t
    dtc_ref[...] = dt_t.T
    acsc_ref[...] = acs_t.T

    buf_ref[SUBLANES:SUBLANES + tm, :] = jnp.dot(h2, win_ref[:, 0:CONV_DIM], preferred_element_type=F32)
    vt = lax.dot_general(wvt_ref[...], h2, (((1,), (1,)), ((), ())), preferred_element_type=F32)
    vt_ref[...] = (vt + vones_ref[...]).astype(BF16)
    o = CONV_DIM
    for ref in (z_ref, q_ref, k_ref):
        n = ref.shape[-1]
        ref[...] = jnp.dot(h2, win_ref[:, o:o + n], preferred_element_type=F32).astype(ref.dtype)
        o += n

    acc = cb_ref[...]
    for k in range(CONV_WIDTH):
        off = SUBLANES - (CONV_WIDTH - 1) + k
        acc = acc + buf_ref[off:off + tm, :] * cw_ref[k:k + 1, :]
    xbc = _silu(acc)
    xs_ref[...] = xbc[:, :SSD_INNER]
    bc_ref[...] = xbc[:, SSD_INNER:].astype(BF16)


def _const_spec(shape):
    return pl.BlockSpec(shape, lambda *_: (0,) * len(shape), pipeline_mode=pl.Buffered(1))


def _ffn1_proj(x, seqlen, n1, wg, wu, wd, nm, win, wvt, vones, cw, cb, dtb, a):
    m = x.shape[0]
    tm = TM_FFN1
    bc_dim = CONV_DIM - SSD_INNER
    row = lambda n: pl.BlockSpec((tm, n), lambda i: (i, 0))
    col = lambda n: pl.BlockSpec((n, tm), lambda i: (0, i))
    return pl.pallas_call(
        functools.partial(_ffn1_proj_kernel, seqlen // tm),
        grid=(m // tm,),
        in_specs=[row(D_MODEL), _const_spec((1, D_MODEL)), _const_spec((D_MODEL, D_FF)),
                  _const_spec((D_MODEL, D_FF)), _const_spec((D_FF, D_MODEL)),
                  _const_spec((1, D_MODEL)), _const_spec((D_MODEL, PROJ_PAD)),
                  _const_spec((VT_ROWS, D_MODEL)), _const_spec((VT_ROWS, 1)),
                  _const_spec((CONV_WIDTH, CONV_DIM)), _const_spec((1, CONV_DIM)),
                  _const_spec((DT_PAD, 1)), _const_spec((DT_PAD, 1))],
        out_specs=[row(D_MODEL), row(SSD_INNER), row(bc_dim), row(SSD_INNER), row(ATT_INNER),
                   row(ATT_INNER), col(VT_ROWS), row(DT_PAD), row(DT_PAD), col(DT_PAD)],
        out_shape=[jax.ShapeDtypeStruct((m, D_MODEL), F32),
                   jax.ShapeDtypeStruct((m, SSD_INNER), F32),
                   jax.ShapeDtypeStruct((m, bc_dim), BF16),
                   jax.ShapeDtypeStruct((m, SSD_INNER), F32),
                   jax.ShapeDtypeStruct((m, ATT_INNER), BF16),
                   jax.ShapeDtypeStruct((m, ATT_INNER), BF16),
                   jax.ShapeDtypeStruct((VT_ROWS, m), BF16),
                   jax.ShapeDtypeStruct((m, DT_PAD), F32),
                   jax.ShapeDtypeStruct((m, DT_PAD), F32),
                   jax.ShapeDtypeStruct((DT_PAD, m), F32)],
        scratch_shapes=[pltpu.VMEM((tm, D_FF), BF16), pltpu.VMEM((tm + SUBLANES, CONV_DIM), F32)],
        compiler_params=pltpu.CompilerParams(dimension_semantics=("arbitrary",),
                                             vmem_limit_bytes=VMEM_LIMIT),
        name="ffn1_proj",
    )(x, n1, wg, wu, wd, nm, win, wvt, vones, cw, cb, dtb, a)


def _ssd_kernel(xs_ref, bc_ref, dtc_ref, acsc_ref, acst_ref, z_ref, dsk_ref, nw_ref, y_ref, h_ref):
    t = T_SSD
    hb = t // 2

    @pl.when(pl.program_id(1) == 0)
    def _():
        h_ref[...] = jnp.zeros_like(h_ref)

    gn = SSD_GROUPS * SSD_STATE
    xs = xs_ref[0]
    bm_bf = bc_ref[0, :, 0:gn]
    cm_bf = bc_ref[0, :, gn:2 * gn]
    dt_c = dtc_ref[0]
    acs_c = acsc_ref[0]
    acs_t = acst_ref[...]

    causal = (lax.broadcasted_iota(jnp.int32, (hb, hb), 0) >= lax.broadcasted_iota(jnp.int32, (hb, hb), 1))
    first = lambda rows: lax.broadcasted_iota(jnp.int32, (rows, 2 * SSD_HEAD_DIM), 1) < SSD_HEAD_DIM
    lo, lo_st, lo_row = first(t), first(SSD_STATE), first(1)
    pw = 2 * SSD_HEAD_DIM
    ys = []
    for g in range(SSD_GROUPS):
        n0 = g * SSD_STATE
        cg = cm_bf[:, n0:n0 + SSD_STATE]
        cbm = lax.dot_general(cg, bm_bf[:, n0:n0 + SSD_STATE], (((1,), (1,)), ((), ())),
                              preferred_element_type=F32)
        bt = bm_bf[:, n0:n0 + SSD_STATE].astype(F32).T
        hprev = h_ref[g]
        yoff = jnp.dot(cg, hprev.astype(BF16), preferred_element_type=F32)
        for pr in range(HEADS_PER_GROUP // 2):
            h0 = g * HEADS_PER_GROUP + 2 * pr
            h1 = h0 + 1
            l0 = pr * pw
            x_pair = xs[:, h0 * SSD_HEAD_DIM:h0 * SSD_HEAD_DIM + pw]
            xdt = (x_pair * jnp.where(lo, dt_c[:, h0:h0 + 1], dt_c[:, h1:h1 + 1])).astype(BF16)
            lhs = []
            for hd in (h0, h1):
                col = acs_c[:, hd:hd + 1]
                row = acs_t[hd:hd + 1, :]
                d00 = jnp.exp2(jnp.where(causal, col[:hb] - row[:, :hb], -jnp.inf))
                d10 = jnp.exp2(col[hb:] - row[:, :hb])
                d11 = jnp.exp2(jnp.where(causal, col[hb:] - row[:, hb:], -jnp.inf))
                top = jnp.concatenate([(cbm[:hb, :hb] * d00).astype(BF16), jnp.zeros((hb, hb), BF16)], axis=1)
                bot = jnp.concatenate([(cbm[hb:, :hb] * d10).astype(BF16),
                                       (cbm[hb:, hb:] * d11).astype(BF16)], axis=1)
                lhs += [top, bot]
            lasts = [acs_t[hd:hd + 1, t - 1:t] for hd in (h0, h1)]
            for hd, last in zip((h0, h1), lasts):
                lhs.append((bt * jnp.exp2(last - acs_t[hd:hd + 1, :])).astype(BF16))
            r = jnp.dot(jnp.concatenate(lhs, axis=0), xdt, preferred_element_type=F32)
            yd = jnp.where(lo, r[0:t], r[t:2 * t])
            st = jnp.where(lo_st, r[2 * t:2 * t + SSD_STATE], r[2 * t + SSD_STATE:])
            ecol = jnp.exp2(jnp.where(lo, acs_c[:, h0:h0 + 1], acs_c[:, h1:h1 + 1]))
            ys.append(yd + yoff[:, l0:l0 + pw] * ecol)
            elast = jnp.exp2(jnp.where(lo_row, lasts[0], lasts[1]))
            h_ref[g, :, l0:l0 + pw] = hprev[:, l0:l0 + pw] * elast + st

    y = jnp.concatenate(ys, axis=1) + xs * dsk_ref[...]
    y = y * _silu(z_ref[0])
    y_ref[0] = _rms(y, nw_ref[...]).astype(y_ref.dtype)


def _ssd(xs, bc, dtc, acsc, acst, z, dsk, nw):
    bsz, seqlen, _ = xs.shape
    t = T_SSD
    nt = seqlen // t
    tok = lambda n: pl.BlockSpec((1, t, n), lambda b, i: (b, i, 0))
    par = lambda shape: pl.BlockSpec(shape, lambda b, i: (0,) * len(shape))
    return pl.pallas_call(
        _ssd_kernel,
        grid=(bsz, nt),
        in_specs=[tok(SSD_INNER), tok(CONV_DIM - SSD_INNER), tok(DT_PAD), tok(DT_PAD),
                  pl.BlockSpec((DT_PAD, t), lambda b, i: (0, b * nt + i)), tok(SSD_INNER),
                  par((1, SSD_INNER)), par((1, SSD_INNER))],
        out_specs=tok(SSD_INNER),
        out_shape=jax.ShapeDtypeStruct((bsz, seqlen, SSD_INNER), BF16),
        scratch_shapes=[pltpu.VMEM((SSD_GROUPS, SSD_STATE, HEADS_PER_GROUP * SSD_HEAD_DIM), F32)],
        compiler_params=pltpu.CompilerParams(dimension_semantics=("parallel", "arbitrary"),
                                             vmem_limit_bytes=VMEM_LIMIT),
        name="ssd",
    )(xs, bc, dtc, acsc, acst, z, dsk, nw)


def _attn_kernel(q_ref, k0_ref, k1_ref, k2_ref, vt0_ref, vt1_ref, vt2_ref, ring_ref, o_ref, st_ref,
                 bias_ref):
    tq = TQ_ATT
    nk = KV_BLOCKS * tq
    lt = pl.program_id(1)

    @pl.when((pl.program_id(0) == 0) & (lt == 0))
    def _():
        kpos = lax.broadcasted_iota(jnp.int32, (nk, tq), 0)
        qpos = lax.broadcasted_iota(jnp.int32, (nk, tq), 1) + (KV_BLOCKS - 1) * tq
        dchunk = qpos // CHUNK - kpos // CHUNK
        neg = jnp.finfo(F32).min
        for hd in range(ATT_HEADS):
            ring = jnp.broadcast_to(ring_ref[hd:hd + 1, :], (nk, ATT_RING))
            toep = pltpu.roll(ring, 0, 1, stride=1, stride_axis=0)[:, 0:tq]
            band = jnp.where(dchunk >= 0, jnp.where(dchunk <= LEFT_CHUNKS, toep, neg), neg)
            for n in range(KV_BLOCKS):
                bias_ref[n, hd] = jnp.where(kpos >= (KV_BLOCKS - 1 - n) * tq, band, neg)

    case = jnp.minimum(lt, KV_BLOCKS - 1)
    pw = 2 * ATT_HEAD_DIM
    lo = lax.broadcasted_iota(jnp.int32, (tq, pw), 1) < ATT_HEAD_DIM
    nt = (((1,), (1,)), ((), ()))
    for hp in range(ATT_HEADS // 2):
        c0 = hp * pw
        qp = q_ref[0, :, c0:c0 + pw]
        zero = jnp.zeros_like(qp)
        qs = jnp.concatenate([jnp.where(lo, qp, zero), jnp.where(lo, zero, qp)], axis=0)
        for j, kr in enumerate((k0_ref, k1_ref, k2_ref)):
            st_ref[hp, j] = lax.dot_general(kr[0, :, c0:c0 + pw], qs, nt, preferred_element_type=F32)
    for hp in range(ATT_HEADS // 2):
        c0 = hp * pw
        outs = []
        for e in range(2):
            hd = 2 * hp + e
            m = None
            ot = None
            for j, vr in reversed(list(enumerate((vt0_ref, vt1_ref, vt2_ref)))):
                s = st_ref[hp, j, :, e * tq:(e + 1) * tq] + bias_ref[case, hd, j * tq:(j + 1) * tq, :]
                mj = jnp.max(s, axis=0, keepdims=True)
                m_new = mj if m is None else jnp.maximum(m, mj)
                p = jnp.exp((s - m_new).astype(BF16))
                oj = jnp.dot(vr[hd * VT_HEAD_ROWS:(hd + 1) * VT_HEAD_ROWS, :], p,
                             preferred_element_type=F32)
                ot = oj if ot is None else ot * jnp.exp(m - m_new) + oj
                m = m_new
            outs.append(ot[0:ATT_HEAD_DIM] / ot[ATT_HEAD_DIM:ATT_HEAD_DIM + 1])
        o_ref[0, :, c0:c0 + pw] = jnp.concatenate(outs, axis=0).T.astype(o_ref.dtype)


def _attn_bias_ring(rel_bias):
    q0 = (KV_BLOCKS - 1) * TQ_ATT
    j = np.arange(ATT_RING)
    e = np.where(j <= TQ_ATT, -j, ATT_RING - j)
    idx = np.clip(q0 - e, -MAX_REL, MAX_REL) + MAX_REL
    lo, hi = q0 - MAX_REL + 1, ATT_RING - (q0 - MAX_REL)
    assert np.all(idx[:lo] == 2 * MAX_REL) and np.all(idx[hi:] == 2 * MAX_REL)
    assert np.all(idx[lo:hi] == np.arange(1, hi - lo + 1))
    rb = rel_bias.astype(F32)
    far = rb[:, 2 * MAX_REL:]
    return jnp.concatenate([jnp.repeat(far, lo, axis=1), rb[:, 1:hi - lo + 1],
                            jnp.repeat(far, ATT_RING - hi, axis=1)], axis=1)


def _attn(q, k, vt, ring):
    bsz, seqlen, _ = q.shape
    tq = TQ_ATT
    nt = seqlen // tq
    blk = lambda back: pl.BlockSpec((1, tq, ATT_INNER), lambda b, i: (b, jnp.maximum(i - back, 0), 0))
    vblk = lambda back: pl.BlockSpec((VT_ROWS, tq), lambda b, i: (0, b * nt + jnp.maximum(i - back, 0)))
    return pl.pallas_call(
        _attn_kernel,
        grid=(bsz, nt),
        in_specs=[blk(0), blk(2), blk(1), blk(0), vblk(2), vblk(1), vblk(0),
                  pl.BlockSpec((ATT_HEADS, ATT_RING), lambda b, i: (0, 0))],
        out_specs=blk(0),
        out_shape=jax.ShapeDtypeStruct((bsz, seqlen, ATT_INNER), BF16),
        scratch_shapes=[pltpu.VMEM((ATT_HEADS // 2, KV_BLOCKS, tq, 2 * tq), F32),
                        pltpu.VMEM((KV_BLOCKS, ATT_HEADS, KV_BLOCKS * tq, tq), F32)],
        compiler_params=pltpu.CompilerParams(dimension_semantics=("arbitrary", "arbitrary"),
                                             vmem_limit_bytes=VMEM_LIMIT),
        name="attn",
    )(q, k, k, k, vt, vt, vt, ring)


def _out_ffn2_kernel(x1_ref, y_ref, att_ref, wo_ref, n2_ref, wg_ref, wu_ref, wd_ref, nf_ref,
                     out_ref, act_ref):
    mix = jnp.dot(y_ref[...], wo_ref[0:SSD_INNER, :], preferred_element_type=F32)
    mix = mix + jnp.dot(att_ref[...], wo_ref[SSD_INNER:MIX_WIDTH, :], preferred_element_type=F32)
    x2 = x1_ref[...] + mix
    h = _rms(x2, n2_ref[...]).astype(BF16)
    x3 = x2 + 0.5 * _swiglu(h, wg_ref, wu_ref, wd_ref, act_ref)
    out_ref[...] = _rms(x3, nf_ref[...])


def _out_ffn2(x1, y, att, wo, n2, wg, wu, wd, nf):
    m = x1.shape[0]
    tm = TM_FFN2
    row = lambda n: pl.BlockSpec((tm, n), lambda i: (i, 0))
    return pl.pallas_call(
        _out_ffn2_kernel,
        grid=(m // tm,),
        in_specs=[row(D_MODEL), row(SSD_INNER), row(ATT_INNER), _const_spec((MIX_WIDTH, D_MODEL)),
                  _const_spec((1, D_MODEL)), _const_spec((D_MODEL, D_FF)), _const_spec((D_MODEL, D_FF)),
                  _const_spec((D_FF, D_MODEL)), _const_spec((1, D_MODEL))],
        out_specs=row(D_MODEL),
        out_shape=jax.ShapeDtypeStruct((m, D_MODEL), F32),
        scratch_shapes=[pltpu.VMEM((tm, D_FF), BF16)],
        compiler_params=pltpu.CompilerParams(dimension_semantics=("parallel",),
                                             vmem_limit_bytes=VMEM_LIMIT),
        name="out_ffn2",
    )(x1, y, att, wo, n2, wg, wu, wd, nf)


def _layer(x, ffn1_norm, ffn1_w_gate, ffn1_w_up, ffn1_w_down, mix_norm, w_in, conv_w, conv_b, dt_bias,
           a_log, d_skip, ssd_norm, rel_bias, w_out, ffn2_norm, ffn2_w_gate, ffn2_w_up, ffn2_w_down,
           out_norm):
    bsz, seqlen, _ = x.shape
    m = bsz * seqlen
    o_xbc = SSD_INNER
    o_dt = o_xbc + CONV_DIM
    o_q = o_dt + SSD_HEADS
    o_k = o_q + ATT_INNER
    o_v = o_k + ATT_INNER
    win = jnp.concatenate(
        [w_in[:, o_xbc:o_dt], w_in[:, :o_xbc], w_in[:, o_q:o_k] * ATT_HEAD_DIM ** -0.5, w_in[:, o_k:o_v],
         w_in[:, o_dt:o_q], jnp.zeros((D_MODEL, DT_PAD - SSD_HEADS), w_in.dtype)], axis=1).astype(BF16)
    pad_rows = ((0, 0), (0, VT_HEAD_ROWS - ATT_HEAD_DIM), (0, 0))
    wvt = jnp.pad(w_in[:, o_v:].T.reshape(ATT_HEADS, ATT_HEAD_DIM, D_MODEL), pad_rows)
    wvt = wvt.reshape(VT_ROWS, D_MODEL).astype(BF16)
    vones = jnp.zeros((ATT_HEADS, VT_HEAD_ROWS), F32).at[:, ATT_HEAD_DIM].set(1.0).reshape(VT_ROWS, 1)
    row = lambda p: p.reshape(1, -1).astype(F32)
    pad_col = lambda p: jnp.pad(p.astype(F32), (0, DT_PAD - SSD_HEADS)).reshape(DT_PAD, 1)

    a_log2 = -jnp.exp(a_log.astype(F32)) * math.log2(math.e)
    x1, xs, bc, z, q, k, vt, dtc, acsc, acst = _ffn1_proj(
        x.reshape(m, D_MODEL), seqlen, row(ffn1_norm), ffn1_w_gate.astype(BF16), ffn1_w_up.astype(BF16),
        ffn1_w_down.astype(BF16), row(mix_norm), win, wvt, vones, conv_w.astype(F32), row(conv_b),
        pad_col(dt_bias), pad_col(a_log2))

    seq = lambda arr: arr.reshape(bsz, seqlen, arr.shape[-1])
    y = _ssd(seq(xs), seq(bc), seq(dtc), seq(acsc), acst, seq(z),
             row(jnp.repeat(d_skip, SSD_HEAD_DIM)), row(ssd_norm))
    att = _attn(seq(q), seq(k), vt, _attn_bias_ring(rel_bias))

    out = _out_ffn2(x1, y.reshape(m, SSD_INNER), att.reshape(m, ATT_INNER), w_out.astype(BF16),
                    row(ffn2_norm), ffn2_w_gate.astype(BF16), ffn2_w_up.astype(BF16),
                    ffn2_w_down.astype(BF16), row(out_norm))
    return out.reshape(bsz, seqlen, D_MODEL)


def kernel(x, ffn1_norm, ffn1_w_gate, ffn1_w_up, ffn1_w_down, mix_norm, w_in, conv_w, conv_b, dt_bias, a_log, d_skip, ssd_norm, rel_bias, w_out, ffn2_norm, ffn2_w_gate, ffn2_w_up, ffn2_w_down, final_norm):
    depth = ffn1_norm.shape[0]
    assert depth == 1, "the fused out_ffn2 kernel applies the final norm after the only layer"
    return _layer(x, ffn1_norm[0], ffn1_w_gate[0], ffn1_w_up[0], ffn1_w_down[0], mix_norm[0], w_in[0],
                  conv_w[0], conv_b[0], dt_bias[0], a_log[0], d_skip[0], ssd_norm[0], rel_bias[0],
                  w_out[0], ffn2_norm[0], ffn2_w_gate[0], ffn2_w_up[0], ffn2_w_down[0], final_norm)
```

```python
import functools
import math

import jax
import jax.numpy as jnp
import numpy as np
from jax import lax
from jax.experimental import pallas as pl
from jax.experimental.pallas import tpu as pltpu

D_MODEL = 1024
CHUNK = 64
LEFT_CHUNKS = 8
EPS = 1e-5
D_FF = 2816
SSD_HEADS = 16
SSD_HEAD_DIM = 64
SSD_INNER = SSD_HEADS * SSD_HEAD_DIM
SSD_GROUPS = 2
HEADS_PER_GROUP = SSD_HEADS // SSD_GROUPS
SSD_STATE = 128
CONV_WIDTH = 4
CONV_DIM = SSD_INNER + 2 * SSD_GROUPS * SSD_STATE
ATT_HEADS = 8
ATT_HEAD_DIM = 64
ATT_INNER = ATT_HEADS * ATT_HEAD_DIM
MAX_REL = 256
MIX_WIDTH = SSD_INNER + ATT_INNER

LANES = 128
SUBLANES = 8
DT_PAD = LANES
PROJ_PAD = CONV_DIM + SSD_INNER + 2 * ATT_INNER + DT_PAD
BF16_SUBLANES = 16
VT_HEAD_ROWS = ATT_HEAD_DIM + BF16_SUBLANES
VT_ROWS = ATT_HEADS * VT_HEAD_ROWS

TM_FFN1 = 256
TM_FFN2 = 512
T_SSD = 256
TQ_ATT = 256
KV_BLOCKS = 3
ATT_RING = (KV_BLOCKS + 1) * TQ_ATT
F_CHUNKS = ((0, 1024), (1024, 2048), (2048, D_FF))
VMEM_LIMIT = 56 * 1024 * 1024

F32 = jnp.float32
BF16 = jnp.bfloat16


def _rms(x, w):
    return x * lax.rsqrt(jnp.mean(x * x, axis=-1, keepdims=True) + EPS) * w


def _silu(x):
    return x * (1.0 / (1.0 + jnp.exp(-x)))


def _swiglu(h, wg_ref, wu_ref, wd_ref, act_ref):
    for lo, hi in F_CHUNKS:
        g = jnp.dot(h, wg_ref[:, lo:hi], preferred_element_type=F32)
        u = jnp.dot(h, wu_ref[:, lo:hi], preferred_element_type=F32)
        act_ref[:, lo:hi] = (_silu(g) * u).astype(BF16)
    return jnp.dot(act_ref[...], wd_ref[...], preferred_element_type=F32)


def _ffn1_proj_kernel(tiles_per_seq, x_ref, n1_ref, wg_ref, wu_ref, wd_ref, nm_ref, win_ref, wvt_ref,
                      vones_ref, cw_ref, cb_ref, dtb_ref, a_ref, x1_ref, xs_ref, bc_ref, z_ref, q_ref,
                      k_ref, vt_ref, dtc_ref, acsc_ref, acst_ref, act_ref, buf_ref):
    tm = TM_FFN1
    first_of_seq = pl.program_id(0) % tiles_per_seq == 0

    @pl.when(first_of_seq)
    def _():
        buf_ref[0:SUBLANES, :] = jnp.zeros((SUBLANES, CONV_DIM), F32)

    @pl.when(jnp.logical_not(first_of_seq))
    def _():
        buf_ref[0:SUBLANES, :] = buf_ref[tm:tm + SUBLANES, :]

    x = x_ref[...]
    h = _rms(x, n1_ref[...]).astype(BF16)
    x1 = x + 0.5 * _swiglu(h, wg_ref, wu_ref, wd_ref, act_ref)
    x1_ref[...] = x1
    h2 = _rms(x1, nm_ref[...]).astype(BF16)

    o = PROJ_PAD - DT_PAD
    dtc_ref[...] = jnp.dot(h2, win_ref[:, o:o + DT_PAD], preferred_element_type=F32)
    xr = dtc_ref[...].T + dtb_ref[...]
    dt_t = jnp.maximum(xr, 0.0) + jnp.log1p(jnp.exp(-jnp.abs(xr)))
    acs_t = dt_t * a_ref[...]
    pos = lax.broadcasted_iota(jnp.int32, (DT_PAD, tm), 1)
    shift = 1
    while shift < tm:
        acs_t = acs_t + jnp.where(pos >= shift, pltpu.roll(acs_t, shift, 1), 0.0)
        shift *= 2
    acst_ref[...] = acs_t
    dtc_ref[...] = dt_t.T
    acsc_ref[...] = acs_t.T

    buf_ref[SUBLANES:SUBLANES + tm, :] = jnp.dot(h2, win_ref[:, 0:CONV_DIM], preferred_element_type=F32)
    vt = lax.dot_general(wvt_ref[...], h2, (((1,), (1,)), ((), ())), preferred_element_type=F32)
    vt_ref[...] = (vt + vones_ref[...]).astype(BF16)
    o = CONV_DIM
    for ref in (z_ref, q_ref, k_ref):
        n = ref.shape[-1]
        ref[...] = jnp.dot(h2, win_ref[:, o:o + n], preferred_element_type=F32).astype(ref.dtype)
        o += n

    acc = cb_ref[...]
    for k in range(CONV_WIDTH):
        off = SUBLANES - (CONV_WIDTH - 1) + k
        acc = acc + buf_ref[off:off + tm, :] * cw_ref[k:k + 1, :]
    xbc = _silu(acc)
    xs_ref[...] = xbc[:, :SSD_INNER]
    bc_ref[...] = xbc[:, SSD_INNER:].astype(BF16)


def _const_spec(shape):
    return pl.BlockSpec(shape, lambda *_: (0,) * len(shape), pipeline_mode=pl.Buffered(1))


def _ffn1_proj(x, seqlen, n1, wg, wu, wd, nm, win, wvt, vones, cw, cb, dtb, a):
    m = x.shape[0]
    tm = TM_FFN1
    bc_dim = CONV_DIM - SSD_INNER
    row = lambda n: pl.BlockSpec((tm, n), lambda i: (i, 0))
    col = lambda n: pl.BlockSpec((n, tm), lambda i: (0, i))
    return pl.pallas_call(
        functools.partial(_ffn1_proj_kernel, seqlen // tm),
        grid=(m // tm,),
        in_specs=[row(D_MODEL), _const_spec((1, D_MODEL)), _const_spec((D_MODEL, D_FF)),
                  _const_spec((D_MODEL, D_FF)), _const_spec((D_FF, D_MODEL)),
                  _const_spec((1, D_MODEL)), _const_spec((D_MODEL, PROJ_PAD)),
                  _const_spec((VT_ROWS, D_MODEL)), _const_spec((VT_ROWS, 1)),
                  _const_spec((CONV_WIDTH, CONV_DIM)), _const_spec((1, CONV_DIM)),
                  _const_spec((DT_PAD, 1)), _const_spec((DT_PAD, 1))],
        out_specs=[row(D_MODEL), row(SSD_INNER), row(bc_dim), row(SSD_INNER), row(ATT_INNER),
                   row(ATT_INNER), col(VT_ROWS), row(DT_PAD), row(DT_PAD), col(DT_PAD)],
        out_shape=[jax.ShapeDtypeStruct((m, D_MODEL), F32),
                   jax.ShapeDtypeStruct((m, SSD_INNER), F32),
                   jax.ShapeDtypeStruct((m, bc_dim), BF16),
                   jax.ShapeDtypeStruct((m, SSD_INNER), F32),
                   jax.ShapeDtypeStruct((m, ATT_INNER), BF16),
                   jax.ShapeDtypeStruct((m, ATT_INNER), BF16),
                   jax.ShapeDtypeStruct((VT_ROWS, m), BF16),
                   jax.ShapeDtypeStruct((m, DT_PAD), F32),
                   jax.ShapeDtypeStruct((m, DT_PAD), F32),
                   jax.ShapeDtypeStruct((DT_PAD, m), F32)],
        scratch_shapes=[pltpu.VMEM((tm, D_FF), BF16), pltpu.VMEM((tm + SUBLANES, CONV_DIM), F32)],
        compiler_params=pltpu.CompilerParams(dimension_semantics=("arbitrary",),
                                             vmem_limit_bytes=VMEM_LIMIT),
        name="ffn1_proj",
    )(x, n1, wg, wu, wd, nm, win, wvt, vones, cw, cb, dtb, a)


def _ssd_kernel(xs_ref, bc_ref, dtc_ref, acsc_ref, acst_ref, z_ref, dsk_ref, nw_ref, y_ref, h_ref):
    t = T_SSD
    hb = t // 2

    @pl.when(pl.program_id(1) == 0)
    def _():
        h_ref[...] = jnp.zeros_like(h_ref)

    gn = SSD_GROUPS * SSD_STATE
    xs = xs_ref[0]
    bm_bf = bc_ref[0, :, 0:gn]
    cm_bf = bc_ref[0, :, gn:2 * gn]
    dt_c = dtc_ref[0]
    acs_c = acsc_ref[0]
    acs_t = acst_ref[...]

    causal = (lax.broadcasted_iota(jnp.int32, (hb, hb), 0) >= lax.broadcasted_iota(jnp.int32, (hb, hb), 1))
    first = lambda rows: lax.broadcasted_iota(jnp.int32, (rows, 2 * SSD_HEAD_DIM), 1) < SSD_HEAD_DIM
    lo, lo_st, lo_row = first(t), first(SSD_STATE), first(1)
    pw = 2 * SSD_HEAD_DIM
    ys = []
    for g in range(SSD_GROUPS):
        n0 = g * SSD_STATE
        cg = cm_bf[:, n0:n0 + SSD_STATE]
        cbm = lax.dot_general(cg, bm_bf[:, n0:n0 + SSD_STATE], (((1,), (1,)), ((), ())),
                              preferred_element_type=F32)
        bt = bm_bf[:, n0:n0 + SSD_STATE].astype(F32).T
        hprev = h_ref[g]
        yoff = jnp.dot(cg, hprev.astype(BF16), preferred_element_type=F32)
        for pr in range(HEADS_PER_GROUP // 2):
            h0 = g * HEADS_PER_GROUP + 2 * pr
            h1 = h0 + 1
            l0 = pr * pw
            x_pair = xs[:, h0 * SSD_HEAD_DIM:h0 * SSD_HEAD_DIM + pw]
            xdt = (x_pair * jnp.where(lo, dt_c[:, h0:h0 + 1], dt_c[:, h1:h1 + 1])).astype(BF16)
            lhs = []
            for hd in (h0, h1):
                col = acs_c[:, hd:hd + 1]
                row = acs_t[hd:hd + 1, :]
                d00 = jnp.exp2(jnp.where(causal, col[:hb] - row[:, :hb], -jnp.inf))
                d10 = jnp.exp2(col[hb:] - row[:, :hb])
                d11 = jnp.exp2(jnp.where(causal, col[hb:] - row[:, hb:], -jnp.inf))
                top = jnp.concatenate([(cbm[:hb, :hb] * d00).astype(BF16), jnp.zeros((hb, hb), BF16)], axis=1)
                bot = jnp.concatenate([(cbm[hb:, :hb] * d10).astype(BF16),
                                       (cbm[hb:, hb:] * d11).astype(BF16)], axis=1)
                lhs += [top, bot]
            lasts = [acs_t[hd:hd + 1, t - 1:t] for hd in (h0, h1)]
            for hd, last in zip((h0, h1), lasts):
                lhs.append((bt * jnp.exp2(last - acs_t[hd:hd + 1, :])).astype(BF16))
            r = jnp.dot(jnp.concatenate(lhs, axis=0), xdt, preferred_element_type=F32)
            yd = jnp.where(lo, r[0:t], r[t:2 * t])
            st = jnp.where(lo_st, r[2 * t:2 * t + SSD_STATE], r[2 * t + SSD_STATE:])
            ecol = jnp.exp2(jnp.where(lo, acs_c[:, h0:h0 + 1], acs_c[:, h1:h1 + 1]))
            ys.append(yd + yoff[:, l0:l0 + pw] * ecol)
            elast = jnp.exp2(jnp.where(lo_row, lasts[0], lasts[1]))
            h_ref[g, :, l0:l0 + pw] = hprev[:, l0:l0 + pw] * elast + st

    y = jnp.concatenate(ys, axis=1) + xs * dsk_ref[...]
    y = y * _silu(z_ref[0])
    y_ref[0] = _rms(y, nw_ref[...]).astype(y_ref.dtype)


def _ssd(xs, bc, dtc, acsc, acst, z, dsk, nw):
    bsz, seqlen, _ = xs.shape
    t = T_SSD
    nt = seqlen // t
    tok = lambda n: pl.BlockSpec((1, t, n), lambda b, i: (b, i, 0))
    par = lambda shape: pl.BlockSpec(shape, lambda b, i: (0,) * len(shape))
    return pl.pallas_call(
        _ssd_kernel,
        grid=(bsz, nt),
        in_specs=[tok(SSD_INNER), tok(CONV_DIM - SSD_INNER), tok(DT_PAD), tok(DT_PAD),
                  pl.BlockSpec((DT_PAD, t), lambda b, i: (0, b * nt + i)), tok(SSD_INNER),
                  par((1, SSD_INNER)), par((1, SSD_INNER))],
        out_specs=tok(SSD_INNER),
        out_shape=jax.ShapeDtypeStruct((bsz, seqlen, SSD_INNER), BF16),
        scratch_shapes=[pltpu.VMEM((SSD_GROUPS, SSD_STATE, HEADS_PER_GROUP * SSD_HEAD_DIM), F32)],
        compiler_params=pltpu.CompilerParams(dimension_semantics=("parallel", "arbitrary"),
                                             vmem_limit_bytes=VMEM_LIMIT),
        name="ssd",
    )(xs, bc, dtc, acsc, acst, z, dsk, nw)


def _attn_kernel(q_ref, k0_ref, k1_ref, k2_ref, vt0_ref, vt1_ref, vt2_ref, ring_ref, o_ref, st_ref,
                 bias_ref):
    tq = TQ_ATT
    nk = KV_BLOCKS * tq
    lt = pl.program_id(1)

    @pl.when((pl.program_id(0) == 0) & (lt == 0))
    def _():
        kpos = lax.broadcasted_iota(jnp.int32, (nk, tq), 0)
        qpos = lax.broadcasted_iota(jnp.int32, (nk, tq), 1) + (KV_BLOCKS - 1) * tq
        dchunk = qpos // CHUNK - kpos // CHUNK
        neg = jnp.finfo(F32).min
        for hd in range(ATT_HEADS):
            ring = jnp.broadcast_to(ring_ref[hd:hd + 1, :], (nk, ATT_RING))
            toep = pltpu.roll(ring, 0, 1, stride=1, stride_axis=0)[:, 0:tq]
            band = jnp.where(dchunk >= 0, jnp.where(dchunk <= LEFT_CHUNKS, toep, neg), neg)
            for n in range(KV_BLOCKS):
                bias_ref[n, hd] = jnp.where(kpos >= (KV_BLOCKS - 1 - n) * tq, band, neg)

    case = jnp.minimum(lt, KV_BLOCKS - 1)
    pw = 2 * ATT_HEAD_DIM
    lo = lax.broadcasted_iota(jnp.int32, (tq, pw), 1) < ATT_HEAD_DIM
    nt = (((1,), (1,)), ((), ()))
    for hp in range(ATT_HEADS // 2):
        c0 = hp * pw
        qp = q_ref[0, :, c0:c0 + pw]
        zero = jnp.zeros_like(qp)
        qs = jnp.concatenate([jnp.where(lo, qp, zero), jnp.where(lo, zero, qp)], axis=0)
        for j, kr in enumerate((k0_ref, k1_ref, k2_ref)):
            st_ref[hp, j] = lax.dot_general(kr[0, :, c0:c0 + pw], qs, nt, preferred_element_type=F32)
    for hp in range(ATT_HEADS // 2):
        c0 = hp * pw
        outs = []
        for e in range(2):
            hd = 2 * hp + e
            m = None
            ot = None
            for j, vr in reversed(list(enumerate((vt0_ref, vt1_ref, vt2_ref)))):
                s = st_ref[hp, j, :, e * tq:(e + 1) * tq] + bias_ref[case, hd, j * tq:(j + 1) * tq, :]
                mj = jnp.max(s, axis=0, keepdims=True)
                m_new = mj if m is None else jnp.maximum(m, mj)
                p = jnp.exp((s - m_new).astype(BF16))
                oj = jnp.dot(vr[hd * VT_HEAD_ROWS:(hd + 1) * VT_HEAD_ROWS, :], p,
                             preferred_element_type=F32)
                ot = oj if ot is None else ot * jnp.exp(m - m_new) + oj
                m = m_new
            outs.append(ot[0:ATT_HEAD_DIM] / ot[ATT_HEAD_DIM:ATT_HEAD_DIM + 1])
        o_ref[0, :, c0:c0 + pw] = jnp.concatenate(outs, axis=0).T.astype(o_ref.dtype)


def _attn_bias_ring(rel_bias):
    q0 = (KV_BLOCKS - 1) * TQ_ATT
    j = np.arange(ATT_RING)
    e = np.where(j <= TQ_ATT, -j, ATT_RING - j)
    idx = np.clip(q0 - e, -MAX_REL, MAX_REL) + MAX_REL
    lo, hi = q0 - MAX_REL + 1, ATT_RING - (q0 - MAX_REL)
    assert np.all(idx[:lo] == 2 * MAX_REL) and np.all(idx[hi:] == 2 * MAX_REL)
    assert np.all(idx[lo:hi] == np.arange(1, hi - lo + 1))
    rb = rel_bias.astype(F32)
    far = rb[:, 2 * MAX_REL:]
    return jnp.concatenate([jnp.repeat(far, lo, axis=1), rb[:, 1:hi - lo + 1],
                            jnp.repeat(far, ATT_RING - hi, axis=1)], axis=1)


def _attn(q, k, vt, ring):
    bsz, seqlen, _ = q.shape
    tq = TQ_ATT
    nt = seqlen // tq
    blk = lambda back: pl.BlockSpec((1, tq, ATT_INNER), lambda b, i: (b, jnp.maximum(i - back, 0), 0))
    vblk = lambda back: pl.BlockSpec((VT_ROWS, tq), lambda b, i: (0, b * nt + jnp.maximum(i - back, 0)))
    return pl.pallas_call(
        _attn_kernel,
        grid=(bsz, nt),
        in_specs=[blk(0), blk(2), blk(1), blk(0), vblk(2), vblk(1), vblk(0),
                  pl.BlockSpec((ATT_HEADS, ATT_RING), lambda b, i: (0, 0))],
        out_specs=blk(0),
        out_shape=jax.ShapeDtypeStruct((bsz, seqlen, ATT_INNER), BF16),
        scratch_shapes=[pltpu.VMEM((ATT_HEADS // 2, KV_BLOCKS, tq, 2 * tq), F32),
                        pltpu.VMEM((KV_BLOCKS, ATT_HEADS, KV_BLOCKS * tq, tq), F32)],
        compiler_params=pltpu.CompilerParams(dimension_semantics=("arbitrary", "arbitrary"),
                                             vmem_limit_bytes=VMEM_LIMIT),
        name="attn",
    )(q, k, k, k, vt, vt, vt, ring)


N_SSD_IN = 8
N_ATT_IN = 8


def _mixer_kernel(*refs):
    ssd_in = refs[:N_SSD_IN]
    att_in = refs[N_SSD_IN:N_SSD_IN + N_ATT_IN]
    y_ref, o_ref, h_ref, st_ref, bias_ref = refs[N_SSD_IN + N_ATT_IN:]
    _attn_kernel(*att_in, o_ref, st_ref, bias_ref)
    _ssd_kernel(*ssd_in, y_ref, h_ref)


def _mixer(xs, bc, dtc, acsc, acst, z, dsk, nw, q, k, vt, ring):
    assert T_SSD == TQ_ATT
    bsz, seqlen, _ = xs.shape
    t = T_SSD
    nt = seqlen // t
    tok = lambda n: pl.BlockSpec((1, t, n), lambda b, i: (b, i, 0))
    par = lambda shape: pl.BlockSpec(shape, lambda b, i: (0,) * len(shape))
    blk = lambda back: pl.BlockSpec((1, t, ATT_INNER), lambda b, i: (b, jnp.maximum(i - back, 0), 0))
    vblk = lambda back: pl.BlockSpec((VT_ROWS, t), lambda b, i: (0, b * nt + jnp.maximum(i - back, 0)))
    return pl.pallas_call(
        _mixer_kernel,
        grid=(bsz, nt),
        in_specs=[tok(SSD_INNER), tok(CONV_DIM - SSD_INNER), tok(DT_PAD), tok(DT_PAD),
                  pl.BlockSpec((DT_PAD, t), lambda b, i: (0, b * nt + i)), tok(SSD_INNER),
                  par((1, SSD_INNER)), par((1, SSD_INNER)),
                  blk(0), blk(2), blk(1), blk(0), vblk(2), vblk(1), vblk(0), par((ATT_HEADS, ATT_RING))],
        out_specs=[tok(SSD_INNER), blk(0)],
        out_shape=[jax.ShapeDtypeStruct((bsz, seqlen, SSD_INNER), BF16),
                   jax.ShapeDtypeStruct((bsz, seqlen, ATT_INNER), BF16)],
        scratch_shapes=[pltpu.VMEM((SSD_GROUPS, SSD_STATE, HEADS_PER_GROUP * SSD_HEAD_DIM), F32),
                        pltpu.VMEM((ATT_HEADS // 2, KV_BLOCKS, t, 2 * t), F32),
                        pltpu.VMEM((KV_BLOCKS, ATT_HEADS, KV_BLOCKS * t, t), F32)],
        compiler_params=pltpu.CompilerParams(dimension_semantics=("arbitrary", "arbitrary"),
                                             vmem_limit_bytes=VMEM_LIMIT),
        name="mixer",
    )(xs, bc, dtc, acsc, acst, z, dsk, nw, q, k, k, k, vt, vt, vt, ring)


def _out_ffn2_kernel(x1_ref, y_ref, att_ref, wo_ref, n2_ref, wg_ref, wu_ref, wd_ref, nf_ref,
                     out_ref, act_ref):
    mix = jnp.dot(y_ref[...], wo_ref[0:SSD_INNER, :], preferred_element_type=F32)
    mix = mix + jnp.dot(att_ref[...], wo_ref[SSD_INNER:MIX_WIDTH, :], preferred_element_type=F32)
    x2 = x1_ref[...] + mix
    h = _rms(x2, n2_ref[...]).astype(BF16)
    x3 = x2 + 0.5 * _swiglu(h, wg_ref, wu_ref, wd_ref, act_ref)
    out_ref[...] = _rms(x3, nf_ref[...])


def _out_ffn2(x1, y, att, wo, n2, wg, wu, wd, nf):
    m = x1.shape[0]
    tm = TM_FFN2
    row = lambda n: pl.BlockSpec((tm, n), lambda i: (i, 0))
    return pl.pallas_call(
        _out_ffn2_kernel,
        grid=(m // tm,),
        in_specs=[row(D_MODEL), row(SSD_INNER), row(ATT_INNER), _const_spec((MIX_WIDTH, D_MODEL)),
                  _const_spec((1, D_MODEL)), _const_spec((D_MODEL, D_FF)), _const_spec((D_MODEL, D_FF)),
                  _const_spec((D_FF, D_MODEL)), _const_spec((1, D_MODEL))],
        out_specs=row(D_MODEL),
        out_shape=jax.ShapeDtypeStruct((m, D_MODEL), F32),
        scratch_shapes=[pltpu.VMEM((tm, D_FF), BF16)],
        compiler_params=pltpu.CompilerParams(dimension_semantics=("parallel",),
                                             vmem_limit_bytes=VMEM_LIMIT),
        name="out_ffn2",
    )(x1, y, att, wo, n2, wg, wu, wd, nf)


def _layer(x, ffn1_norm, ffn1_w_gate, ffn1_w_up, ffn1_w_down, mix_norm, w_in, conv_w, conv_b, dt_bias,
           a_log, d_skip, ssd_norm, rel_bias, w_out, ffn2_norm, ffn2_w_gate, ffn2_w_up, ffn2_w_down,
           out_norm):
    bsz, seqlen, _ = x.shape
    m = bsz * seqlen
    o_xbc = SSD_INNER
    o_dt = o_xbc + CONV_DIM
    o_q = o_dt + SSD_HEADS
    o_k = o_q + ATT_INNER
    o_v = o_k + ATT_INNER
    w_bf = w_in.astype(BF16)
    win = jnp.concatenate(
        [w_bf[:, o_xbc:o_dt], w_bf[:, :o_xbc], w_bf[:, o_q:o_k] * ATT_HEAD_DIM ** -0.5, w_bf[:, o_k:o_v],
         w_bf[:, o_dt:o_q], jnp.zeros((D_MODEL, DT_PAD - SSD_HEADS), BF16)], axis=1)
    pad_rows = ((0, 0), (0, VT_HEAD_ROWS - ATT_HEAD_DIM), (0, 0))
    wvt = jnp.pad(w_bf[:, o_v:].T.reshape(ATT_HEADS, ATT_HEAD_DIM, D_MODEL), pad_rows)
    wvt = wvt.reshape(VT_ROWS, D_MODEL)
    vones = jnp.zeros((ATT_HEADS, VT_HEAD_ROWS), F32).at[:, ATT_HEAD_DIM].set(1.0).reshape(VT_ROWS, 1)
    row = lambda p: p.reshape(1, -1).astype(F32)
    pad_col = lambda p: jnp.pad(p.astype(F32), (0, DT_PAD - SSD_HEADS)).reshape(DT_PAD, 1)

    a_log2 = -jnp.exp(a_log.astype(F32)) * math.log2(math.e)
    x1, xs, bc, z, q, k, vt, dtc, acsc, acst = _ffn1_proj(
        x.reshape(m, D_MODEL), seqlen, row(ffn1_norm), ffn1_w_gate.astype(BF16), ffn1_w_up.astype(BF16),
        ffn1_w_down.astype(BF16), row(mix_norm), win, wvt, vones, conv_w.astype(F32), row(conv_b),
        pad_col(dt_bias), pad_col(a_log2))

    seq = lambda arr: arr.reshape(bsz, seqlen, arr.shape[-1])
    y, att = _mixer(seq(xs), seq(bc), seq(dtc), seq(acsc), acst, seq(z),
                    row(jnp.repeat(d_skip, SSD_HEAD_DIM)), row(ssd_norm),
                    seq(q), seq(k), vt, _attn_bias_ring(rel_bias))

    out = _out_ffn2(x1, y.reshape(m, SSD_INNER), att.reshape(m, ATT_INNER), w_out.astype(BF16),
                    row(ffn2_norm), ffn2_w_gate.astype(BF16), ffn2_w_up.astype(BF16),
                    ffn2_w_down.astype(BF16), row(out_norm))
    return out.reshape(bsz, seqlen, D_MODEL)


def kernel(x, ffn1_norm, ffn1_w_gate, ffn1_w_up, ffn1_w_down, mix_norm, w_in, conv_w, conv_b, dt_bias, a_log, d_skip, ssd_norm, rel_bias, w_out, ffn2_norm, ffn2_w_gate, ffn2_w_up, ffn2_w_down, final_norm):
    depth = ffn1_norm.shape[0]
    assert depth == 1, "the fused out_ffn2 kernel applies the final norm after the only layer"
    return _layer(x, ffn1_norm[0], ffn1_w_gate[0], ffn1_w_up[0], ffn1_w_down[0], mix_norm[0], w_in[0],
                  conv_w[0], conv_b[0], dt_bias[0], a_log[0], d_skip[0], ssd_norm[0], rel_bias[0],
                  w_out[0], ffn2_norm[0], ffn2_w_gate[0], ffn2_w_up[0], ffn2_w_down[0], final_norm)
```

```python
import math

import jax
import jax.numpy as jnp
import numpy as np
from jax import lax
from jax.experimental import pallas as pl
from jax.experimental.pallas import tpu as pltpu

D_MODEL = 1024
CHUNK = 64
LEFT_CHUNKS = 8
EPS = 1e-5
D_FF = 2816
SSD_HEADS = 16
SSD_HEAD_DIM = 64
SSD_INNER = SSD_HEADS * SSD_HEAD_DIM
SSD_GROUPS = 2
HEADS_PER_GROUP = SSD_HEADS // SSD_GROUPS
SSD_STATE = 128
CONV_WIDTH = 4
CONV_DIM = SSD_INNER + 2 * SSD_GROUPS * SSD_STATE
ATT_HEADS = 8
ATT_HEAD_DIM = 64
ATT_INNER = ATT_HEADS * ATT_HEAD_DIM
MAX_REL = 256
MIX_WIDTH = SSD_INNER + ATT_INNER

LANES = 128
SUBLANES = 8
DT_PAD = LANES
PROJ_PAD = CONV_DIM + SSD_INNER + 2 * ATT_INNER + DT_PAD
BF16_SUBLANES = 16
VT_HEAD_ROWS = ATT_HEAD_DIM + BF16_SUBLANES
VT_ROWS = ATT_HEADS * VT_HEAD_ROWS

TM_FFN1 = 512
TM_FFN2 = 512
T_SSD = 256
TQ_ATT = 256
KV_BLOCKS = 3
ATT_RING = (KV_BLOCKS + 1) * TQ_ATT
F_CHUNKS = ((0, 1024), (1024, 2048), (2048, D_FF))
VMEM_LIMIT = 56 * 1024 * 1024

F32 = jnp.float32
BF16 = jnp.bfloat16


def _rms(x, w):
    return x * lax.rsqrt(jnp.mean(x * x, axis=-1, keepdims=True) + EPS) * w


def _silu(x):
    return x * (1.0 / (1.0 + jnp.exp(-x)))


def _swiglu(h, wg_ref, wu_ref, wd_ref, act_ref):
    for lo, hi in F_CHUNKS:
        g = jnp.dot(h, wg_ref[:, lo:hi], preferred_element_type=F32)
        u = jnp.dot(h, wu_ref[:, lo:hi], preferred_element_type=F32)
        act_ref[:, lo:hi] = (_silu(g) * u).astype(BF16)
    return jnp.dot(act_ref[...], wd_ref[...], preferred_element_type=F32)


def _ffn1_kernel(x_ref, n1_ref, wg_ref, wu_ref, wd_ref, x1_ref, act_ref):
    x = x_ref[...]
    h = _rms(x, n1_ref[...]).astype(BF16)
    x1_ref[...] = x + 0.5 * _swiglu(h, wg_ref, wu_ref, wd_ref, act_ref)


def _ffn1(x, n1, wg, wu, wd):
    m = x.shape[0]
    tm = TM_FFN1
    row = lambda n: pl.BlockSpec((tm, n), lambda i: (i, 0))
    return pl.pallas_call(
        _ffn1_kernel,
        grid=(m // tm,),
        in_specs=[row(D_MODEL), _const_spec((1, D_MODEL)), _const_spec((D_MODEL, D_FF)),
                  _const_spec((D_MODEL, D_FF)), _const_spec((D_FF, D_MODEL))],
        out_specs=row(D_MODEL),
        out_shape=jax.ShapeDtypeStruct((m, D_MODEL), F32),
        scratch_shapes=[pltpu.VMEM((tm, D_FF), BF16)],
        compiler_params=pltpu.CompilerParams(dimension_semantics=("parallel",),
                                             vmem_limit_bytes=VMEM_LIMIT),
        name="ffn1",
    )(x, n1, wg, wu, wd)


def _proj_body(first_of_seq, x1_ref, nm_ref, win_ref, wvt_ref, vones_ref, cw_ref, cb_ref, dtb_ref,
               a_ref, xs_ref, bc_ref, z_ref, q_ref, k_ref, vt_ref, dtc_ref, acsc_ref, acst_ref, buf_ref):
    tm = T_SSD

    @pl.when(first_of_seq)
    def _():
        buf_ref[0:SUBLANES, :] = jnp.zeros((SUBLANES, CONV_DIM), F32)

    @pl.when(jnp.logical_not(first_of_seq))
    def _():
        buf_ref[0:SUBLANES, :] = buf_ref[tm:tm + SUBLANES, :]

    h2 = _rms(x1_ref[...], nm_ref[...]).astype(BF16)

    o = PROJ_PAD - DT_PAD
    dtc_ref[...] = jnp.dot(h2, win_ref[:, o:o + DT_PAD], preferred_element_type=F32)
    xr = dtc_ref[...].T + dtb_ref[...]
    dt_t = jnp.maximum(xr, 0.0) + jnp.log1p(jnp.exp(-jnp.abs(xr)))
    acs_t = dt_t * a_ref[...]
    pos = lax.broadcasted_iota(jnp.int32, (DT_PAD, tm), 1)
    shift = 1
    while shift < tm:
        acs_t = acs_t + jnp.where(pos >= shift, pltpu.roll(acs_t, shift, 1), 0.0)
        shift *= 2
    acst_ref[...] = acs_t
    dtc_ref[...] = dt_t.T
    acsc_ref[...] = acs_t.T

    buf_ref[SUBLANES:SUBLANES + tm, :] = jnp.dot(h2, win_ref[:, 0:CONV_DIM], preferred_element_type=F32)
    vt = lax.dot_general(wvt_ref[...], h2, (((1,), (1,)), ((), ())), preferred_element_type=F32)
    vt_ref[...] = (vt + vones_ref[...]).astype(BF16)
    o = CONV_DIM
    for ref in (z_ref, q_ref, k_ref):
        n = ref.shape[-1]
        ref[...] = jnp.dot(h2, win_ref[:, o:o + n], preferred_element_type=F32).astype(ref.dtype)
        o += n

    acc = cb_ref[...]
    for k in range(CONV_WIDTH):
        off = SUBLANES - (CONV_WIDTH - 1) + k
        acc = acc + buf_ref[off:off + tm, :] * cw_ref[k:k + 1, :]
    xbc = _silu(acc)
    xs_ref[...] = xbc[:, :SSD_INNER]
    bc_ref[...] = xbc[:, SSD_INNER:].astype(BF16)


def _const_spec(shape):
    return pl.BlockSpec(shape, lambda *_: (0,) * len(shape), pipeline_mode=pl.Buffered(1))


def _ssd_body(xs_ref, bc_ref, dtc_ref, acsc_ref, acst_ref, z_ref, dsk_ref, nw_ref, y_ref, h_ref):
    t = T_SSD
    hb = t // 2

    @pl.when(pl.program_id(1) == 0)
    def _():
        h_ref[...] = jnp.zeros_like(h_ref)

    gn = SSD_GROUPS * SSD_STATE
    xs = xs_ref[0]
    bm_bf = bc_ref[0, :, 0:gn]
    cm_bf = bc_ref[0, :, gn:2 * gn]
    dt_c = dtc_ref[0]
    acs_c = acsc_ref[0]
    acs_t = acst_ref[...]

    causal = (lax.broadcasted_iota(jnp.int32, (hb, hb), 0) >= lax.broadcasted_iota(jnp.int32, (hb, hb), 1))
    first = lambda rows: lax.broadcasted_iota(jnp.int32, (rows, 2 * SSD_HEAD_DIM), 1) < SSD_HEAD_DIM
    lo, lo_st, lo_row = first(t), first(SSD_STATE), first(1)
    pw = 2 * SSD_HEAD_DIM
    ys = []
    for g in range(SSD_GROUPS):
        n0 = g * SSD_STATE
        cg = cm_bf[:, n0:n0 + SSD_STATE]
        cbm = lax.dot_general(cg, bm_bf[:, n0:n0 + SSD_STATE], (((1,), (1,)), ((), ())),
                              preferred_element_type=F32)
        bt = bm_bf[:, n0:n0 + SSD_STATE].astype(F32).T
        hprev = h_ref[g]
        yoff = jnp.dot(cg, hprev.astype(BF16), preferred_element_type=F32)
        for pr in range(HEADS_PER_GROUP // 2):
            h0 = g * HEADS_PER_GROUP + 2 * pr
            h1 = h0 + 1
            l0 = pr * pw
            x_pair = xs[:, h0 * SSD_HEAD_DIM:h0 * SSD_HEAD_DIM + pw]
            xdt = (x_pair * jnp.where(lo, dt_c[:, h0:h0 + 1], dt_c[:, h1:h1 + 1])).astype(BF16)
            lhs = []
            for hd in (h0, h1):
                col = acs_c[:, hd:hd + 1]
                row = acs_t[hd:hd + 1, :]
                d00 = jnp.exp2(jnp.where(causal, col[:hb] - row[:, :hb], -jnp.inf))
                d10 = jnp.exp2(col[hb:] - row[:, :hb])
                d11 = jnp.exp2(jnp.where(causal, col[hb:] - row[:, hb:], -jnp.inf))
                top = jnp.concatenate([(cbm[:hb, :hb] * d00).astype(BF16), jnp.zeros((hb, hb), BF16)], axis=1)
                bot = jnp.concatenate([(cbm[hb:, :hb] * d10).astype(BF16),
                                       (cbm[hb:, hb:] * d11).astype(BF16)], axis=1)
                lhs += [top, bot]
            lasts = [acs_t[hd:hd + 1, t - 1:t] for hd in (h0, h1)]
            for hd, last in zip((h0, h1), lasts):
                lhs.append((bt * jnp.exp2(last - acs_t[hd:hd + 1, :])).astype(BF16))
            r = jnp.dot(jnp.concatenate(lhs, axis=0), xdt, preferred_element_type=F32)
            yd = jnp.where(lo, r[0:t], r[t:2 * t])
            st = jnp.where(lo_st, r[2 * t:2 * t + SSD_STATE], r[2 * t + SSD_STATE:])
            ecol = jnp.exp2(jnp.where(lo, acs_c[:, h0:h0 + 1], acs_c[:, h1:h1 + 1]))
            ys.append(yd + yoff[:, l0:l0 + pw] * ecol)
            elast = jnp.exp2(jnp.where(lo_row, lasts[0], lasts[1]))
            h_ref[g, :, l0:l0 + pw] = hprev[:, l0:l0 + pw] * elast + st

    y = jnp.concatenate(ys, axis=1) + xs * dsk_ref[...]
    y = y * _silu(z_ref[0])
    y_ref[0] = _rms(y, nw_ref[...]).astype(y_ref.dtype)


def _attn_body(q_ref, k0_ref, k1_ref, k2_ref, vt0_ref, vt1_ref, vt2_ref, ring_ref, o_ref, st_ref,
               bias_ref):
    tq = TQ_ATT
    nk = KV_BLOCKS * tq
    lt = pl.program_id(1)

    @pl.when((pl.program_id(0) == 0) & (lt == 0))
    def _():
        kpos = lax.broadcasted_iota(jnp.int32, (nk, tq), 0)
        qpos = lax.broadcasted_iota(jnp.int32, (nk, tq), 1) + (KV_BLOCKS - 1) * tq
        dchunk = qpos // CHUNK - kpos // CHUNK
        neg = jnp.finfo(F32).min
        for hd in range(ATT_HEADS):
            ring = jnp.broadcast_to(ring_ref[hd:hd + 1, :], (nk, ATT_RING))
            toep = pltpu.roll(ring, 0, 1, stride=1, stride_axis=0)[:, 0:tq]
            band = jnp.where(dchunk >= 0, jnp.where(dchunk <= LEFT_CHUNKS, toep, neg), neg)
            for n in range(KV_BLOCKS):
                bias_ref[n, hd] = jnp.where(kpos >= (KV_BLOCKS - 1 - n) * tq, band, neg)

    case = jnp.minimum(lt, KV_BLOCKS - 1)
    pw = 2 * ATT_HEAD_DIM
    lo = lax.broadcasted_iota(jnp.int32, (tq, pw), 1) < ATT_HEAD_DIM
    nt = (((1,), (1,)), ((), ()))
    for hp in range(ATT_HEADS // 2):
        c0 = hp * pw
        qp = q_ref[0, :, c0:c0 + pw]
        zero = jnp.zeros_like(qp)
        qs = jnp.concatenate([jnp.where(lo, qp, zero), jnp.where(lo, zero, qp)], axis=0)
        for j, kr in enumerate((k0_ref, k1_ref, k2_ref)):
            st_ref[hp, j] = lax.dot_general(kr[0, :, c0:c0 + pw], qs, nt, preferred_element_type=F32)
    for hp in range(ATT_HEADS // 2):
        c0 = hp * pw
        outs = []
        for e in range(2):
            hd = 2 * hp + e
            m = None
            ot = None
            for j, vr in reversed(list(enumerate((vt0_ref, vt1_ref, vt2_ref)))):
                s = st_ref[hp, j, :, e * tq:(e + 1) * tq] + bias_ref[case, hd, j * tq:(j + 1) * tq, :]
                mj = jnp.max(s, axis=0, keepdims=True)
                m_new = mj if m is None else jnp.maximum(m, mj)
                p = jnp.exp((s - m_new).astype(BF16))
                oj = jnp.dot(vr[hd * VT_HEAD_ROWS:(hd + 1) * VT_HEAD_ROWS, :], p,
                             preferred_element_type=F32)
                ot = oj if ot is None else ot * jnp.exp(m - m_new) + oj
                m = m_new
            outs.append(ot[0:ATT_HEAD_DIM] / ot[ATT_HEAD_DIM:ATT_HEAD_DIM + 1])
        o_ref[0, :, c0:c0 + pw] = jnp.concatenate(outs, axis=0).T.astype(o_ref.dtype)


def _attn_bias_ring(rel_bias):
    q0 = (KV_BLOCKS - 1) * TQ_ATT
    j = np.arange(ATT_RING)
    e = np.where(j <= TQ_ATT, -j, ATT_RING - j)
    idx = np.clip(q0 - e, -MAX_REL, MAX_REL) + MAX_REL
    lo, hi = q0 - MAX_REL + 1, ATT_RING - (q0 - MAX_REL)
    assert np.all(idx[:lo] == 2 * MAX_REL) and np.all(idx[hi:] == 2 * MAX_REL)
    assert np.all(idx[lo:hi] == np.arange(1, hi - lo + 1))
    rb = rel_bias.astype(F32)
    far = rb[:, 2 * MAX_REL:]
    return jnp.concatenate([jnp.repeat(far, lo, axis=1), rb[:, 1:hi - lo + 1],
                            jnp.repeat(far, ATT_RING - hi, axis=1)], axis=1)


def _front_kernel(x1_ref, nm_ref, win_ref, wvt_ref, vones_ref, cw_ref, cb_ref, dtb_ref, a_ref, dsk_ref,
                  nw_ref, ring_ref, y_ref, o_ref, buf_ref, h_ref, st_ref, bias_ref, kring_ref, vtring_ref,
                  xs_ref, bc_ref, z_ref, q_ref, dtc_ref, acsc_ref, acst_ref):
    lt = pl.program_id(1)

    @pl.when((pl.program_id(0) == 0) & (lt == 0))
    def _():
        kring_ref[...] = jnp.zeros_like(kring_ref)
        vtring_ref[...] = jnp.zeros_like(vtring_ref)

    slot = lambda back: (lt + (KV_BLOCKS - back)) % KV_BLOCKS
    _proj_body(lt == 0, x1_ref.at[0], nm_ref, win_ref, wvt_ref, vones_ref, cw_ref, cb_ref, dtb_ref, a_ref,
               xs_ref.at[0], bc_ref.at[0], z_ref.at[0], q_ref.at[0], kring_ref.at[slot(0)],
               vtring_ref.at[slot(0)], dtc_ref.at[0], acsc_ref.at[0], acst_ref, buf_ref)
    keys = [kring_ref.at[pl.ds(slot(back), 1)] for back in (2, 1, 0)]
    vts = [vtring_ref.at[slot(back)] for back in (2, 1, 0)]
    _attn_body(q_ref, *keys, *vts, ring_ref, o_ref, st_ref, bias_ref)
    _ssd_body(xs_ref, bc_ref, dtc_ref, acsc_ref, acst_ref, z_ref, dsk_ref, nw_ref, y_ref, h_ref)


def _front(x1, nm, win, wvt, vones, cw, cb, dtb, a, dsk, nw, ring):
    assert T_SSD == TQ_ATT
    bsz, seqlen, _ = x1.shape
    t = T_SSD
    tok = lambda n: pl.BlockSpec((1, t, n), lambda b, i: (b, i, 0))
    bc_dim = CONV_DIM - SSD_INNER
    return pl.pallas_call(
        _front_kernel,
        grid=(bsz, seqlen // t),
        in_specs=[tok(D_MODEL), _const_spec((1, D_MODEL)), _const_spec((D_MODEL, PROJ_PAD)),
                  _const_spec((VT_ROWS, D_MODEL)), _const_spec((VT_ROWS, 1)),
                  _const_spec((CONV_WIDTH, CONV_DIM)), _const_spec((1, CONV_DIM)),
                  _const_spec((DT_PAD, 1)), _const_spec((DT_PAD, 1)), _const_spec((1, SSD_INNER)),
                  _const_spec((1, SSD_INNER)), _const_spec((ATT_HEADS, ATT_RING))],
        out_specs=[tok(SSD_INNER), tok(ATT_INNER)],
        out_shape=[jax.ShapeDtypeStruct((bsz, seqlen, SSD_INNER), BF16),
                   jax.ShapeDtypeStruct((bsz, seqlen, ATT_INNER), BF16)],
        scratch_shapes=[pltpu.VMEM((t + SUBLANES, CONV_DIM), F32),
                        pltpu.VMEM((SSD_GROUPS, SSD_STATE, HEADS_PER_GROUP * SSD_HEAD_DIM), F32),
                        pltpu.VMEM((ATT_HEADS // 2, KV_BLOCKS, t, 2 * t), F32),
                        pltpu.VMEM((KV_BLOCKS, ATT_HEADS, KV_BLOCKS * t, t), F32),
                        pltpu.VMEM((KV_BLOCKS, t, ATT_INNER), BF16),
                        pltpu.VMEM((KV_BLOCKS, VT_ROWS, t), BF16),
                        pltpu.VMEM((1, t, SSD_INNER), F32), pltpu.VMEM((1, t, bc_dim), BF16),
                        pltpu.VMEM((1, t, SSD_INNER), F32), pltpu.VMEM((1, t, ATT_INNER), BF16),
                        pltpu.VMEM((1, t, DT_PAD), F32), pltpu.VMEM((1, t, DT_PAD), F32),
                        pltpu.VMEM((DT_PAD, t), F32)],
        compiler_params=pltpu.CompilerParams(dimension_semantics=("arbitrary", "arbitrary"),
                                             vmem_limit_bytes=VMEM_LIMIT),
        name="front",
    )(x1, nm, win, wvt, vones, cw, cb, dtb, a, dsk, nw, ring)


def _out_ffn2_kernel(x1_ref, y_ref, att_ref, wo_ref, n2_ref, wg_ref, wu_ref, wd_ref, nf_ref,
                     out_ref, act_ref):
    mix = jnp.dot(y_ref[...], wo_ref[0:SSD_INNER, :], preferred_element_type=F32)
    mix = mix + jnp.dot(att_ref[...], wo_ref[SSD_INNER:MIX_WIDTH, :], preferred_element_type=F32)
    x2 = x1_ref[...] + mix
    h = _rms(x2, n2_ref[...]).astype(BF16)
    x3 = x2 + 0.5 * _swiglu(h, wg_ref, wu_ref, wd_ref, act_ref)
    out_ref[...] = _rms(x3, nf_ref[...])


def _out_ffn2(x1, y, att, wo, n2, wg, wu, wd, nf):
    m = x1.shape[0]
    tm = TM_FFN2
    row = lambda n: pl.BlockSpec((tm, n), lambda i: (i, 0))
    return pl.pallas_call(
        _out_ffn2_kernel,
        grid=(m // tm,),
        in_specs=[row(D_MODEL), row(SSD_INNER), row(ATT_INNER), _const_spec((MIX_WIDTH, D_MODEL)),
                  _const_spec((1, D_MODEL)), _const_spec((D_MODEL, D_FF)), _const_spec((D_MODEL, D_FF)),
                  _const_spec((D_FF, D_MODEL)), _const_spec((1, D_MODEL))],
        out_specs=row(D_MODEL),
        out_shape=jax.ShapeDtypeStruct((m, D_MODEL), F32),
        scratch_shapes=[pltpu.VMEM((tm, D_FF), BF16)],
        compiler_params=pltpu.CompilerParams(dimension_semantics=("parallel",),
                                             vmem_limit_bytes=VMEM_LIMIT),
        name="out_ffn2",
    )(x1, y, att, wo, n2, wg, wu, wd, nf)


def _layer(x, ffn1_norm, ffn1_w_gate, ffn1_w_up, ffn1_w_down, mix_norm, w_in, conv_w, conv_b, dt_bias,
           a_log, d_skip, ssd_norm, rel_bias, w_out, ffn2_norm, ffn2_w_gate, ffn2_w_up, ffn2_w_down,
           out_norm):
    bsz, seqlen, _ = x.shape
    m = bsz * seqlen
    o_xbc = SSD_INNER
    o_dt = o_xbc + CONV_DIM
    o_q = o_dt + SSD_HEADS
    o_k = o_q + ATT_INNER
    o_v = o_k + ATT_INNER
    w_bf = w_in.astype(BF16)
    win = jnp.concatenate(
        [w_bf[:, o_xbc:o_dt], w_bf[:, :o_xbc], w_bf[:, o_q:o_k] * ATT_HEAD_DIM ** -0.5, w_bf[:, o_k:o_v],
         w_bf[:, o_dt:o_q], jnp.zeros((D_MODEL, DT_PAD - SSD_HEADS), BF16)], axis=1)
    pad_rows = ((0, 0), (0, VT_HEAD_ROWS - ATT_HEAD_DIM), (0, 0))
    wvt = jnp.pad(w_bf[:, o_v:].T.reshape(ATT_HEADS, ATT_HEAD_DIM, D_MODEL), pad_rows)
    wvt = wvt.reshape(VT_ROWS, D_MODEL)
    vones = jnp.zeros((ATT_HEADS, VT_HEAD_ROWS), F32).at[:, ATT_HEAD_DIM].set(1.0).reshape(VT_ROWS, 1)
    row = lambda p: p.reshape(1, -1).astype(F32)
    pad_col = lambda p: jnp.pad(p.astype(F32), (0, DT_PAD - SSD_HEADS)).reshape(DT_PAD, 1)

    a_log2 = -jnp.exp(a_log.astype(F32)) * math.log2(math.e)
    x1 = _ffn1(x.reshape(m, D_MODEL), row(ffn1_norm), ffn1_w_gate.astype(BF16), ffn1_w_up.astype(BF16),
               ffn1_w_down.astype(BF16))
    y, att = _front(x1.reshape(bsz, seqlen, D_MODEL), row(mix_norm), win, wvt, vones, conv_w.astype(F32),
                    row(conv_b), pad_col(dt_bias), pad_col(a_log2), row(jnp.repeat(d_skip, SSD_HEAD_DIM)),
                    row(ssd_norm), _attn_bias_ring(rel_bias))

    out = _out_ffn2(x1, y.reshape(m, SSD_INNER), att.reshape(m, ATT_INNER), w_out.astype(BF16),
                    row(ffn2_norm), ffn2_w_gate.astype(BF16), ffn2_w_up.astype(BF16),
                    ffn2_w_down.astype(BF16), row(out_norm))
    return out.reshape(bsz, seqlen, D_MODEL)


def kernel(x, ffn1_norm, ffn1_w_gate, ffn1_w_up, ffn1_w_down, mix_norm, w_in, conv_w, conv_b, dt_bias, a_log, d_skip, ssd_norm, rel_bias, w_out, ffn2_norm, ffn2_w_gate, ffn2_w_up, ffn2_w_down, final_norm):
    depth = ffn1_norm.shape[0]
    assert depth == 1, "the fused out_ffn2 kernel applies the final norm after the only layer"
    return _layer(x, ffn1_norm[0], ffn1_w_gate[0], ffn1_w_up[0], ffn1_w_down[0], mix_norm[0], w_in[0],
                  conv_w[0], conv_b[0], dt_bias[0], a_log[0], d_skip[0], ssd_norm[0], rel_bias[0],
                  w_out[0], ffn2_norm[0], ffn2_w_gate[0], ffn2_w_up[0], ffn2_w_down[0], final_norm)
```

```python
import functools
import math

import jax
import jax.numpy as jnp
import numpy as np
from jax import lax
from jax.experimental import pallas as pl
from jax.experimental.pallas import tpu as pltpu

D_MODEL = 1024
CHUNK = 64
LEFT_CHUNKS = 8
EPS = 1e-5
D_FF = 2816
SSD_HEADS = 16
SSD_HEAD_DIM = 64
SSD_INNER = SSD_HEADS * SSD_HEAD_DIM
SSD_GROUPS = 2
HEADS_PER_GROUP = SSD_HEADS // SSD_GROUPS
SSD_STATE = 128
CONV_WIDTH = 4
CONV_DIM = SSD_INNER + 2 * SSD_GROUPS * SSD_STATE
ATT_HEADS = 8
ATT_HEAD_DIM = 64
ATT_INNER = ATT_HEADS * ATT_HEAD_DIM
MAX_REL = 256
MIX_WIDTH = SSD_INNER + ATT_INNER

LANES = 128
SUBLANES = 8
DT_PAD = LANES
PROJ_PAD = CONV_DIM + SSD_INNER + 2 * ATT_INNER + DT_PAD
BF16_SUBLANES = 16
VT_HEAD_ROWS = ATT_HEAD_DIM + BF16_SUBLANES
VT_ROWS = ATT_HEADS * VT_HEAD_ROWS

TM_FFN1 = 512
TM_FFN2 = 512
T_SSD = 256
TQ_ATT = 256
KV_BLOCKS = 3
ATT_RING = (KV_BLOCKS + 1) * TQ_ATT
DEAD_QUADRANT = {0: (0, 1), KV_BLOCKS - 1: (1, 0)}
F_CHUNKS = ((0, 1024), (1024, 2048), (2048, D_FF))
VMEM_LIMIT = 56 * 1024 * 1024

F32 = jnp.float32
BF16 = jnp.bfloat16


def _rms(x, w):
    return x * lax.rsqrt(jnp.mean(x * x, axis=-1, keepdims=True) + EPS) * w


def _silu(x):
    return x * (1.0 / (1.0 + jnp.exp(-x)))


def _swiglu(h, wg_ref, wu_ref, wd_ref, act_ref):
    for lo, hi in F_CHUNKS:
        g = jnp.dot(h, wg_ref[:, lo:hi], preferred_element_type=F32)
        u = jnp.dot(h, wu_ref[:, lo:hi], preferred_element_type=F32)
        act_ref[:, lo:hi] = (_silu(g) * u).astype(BF16)
    return jnp.dot(act_ref[...], wd_ref[...], preferred_element_type=F32)


def _ffn1_kernel(n_cast, x_ref, n1_ref, wg_ref, wu_ref, wd_ref, *refs):
    cast_in, x1_ref, cast_out, act_ref = refs[:n_cast], refs[n_cast], refs[n_cast + 1:-1], refs[-1]
    x = x_ref[...]
    h = _rms(x, n1_ref[...]).astype(BF16)
    x1_ref[...] = x + 0.5 * _swiglu(h, wg_ref, wu_ref, wd_ref, act_ref)
    for src, dst in zip(cast_in, cast_out):
        dst[...] = src[...].astype(BF16)


def _cast_block_rows(rows, steps):
    return -(-rows // (steps * BF16_SUBLANES)) * BF16_SUBLANES


def _ffn1(x, n1, wg, wu, wd, to_cast):
    m = x.shape[0]
    tm = TM_FFN1
    steps = m // tm
    row = lambda n: pl.BlockSpec((tm, n), lambda i: (i, 0))

    def cast_spec(w):
        rb = _cast_block_rows(w.shape[0], steps)
        last = -(-w.shape[0] // rb) - 1
        return pl.BlockSpec((rb, w.shape[1]), lambda i: (jnp.minimum(i, last), 0))

    cast_specs = [cast_spec(w) for w in to_cast]
    outs = pl.pallas_call(
        functools.partial(_ffn1_kernel, len(to_cast)),
        grid=(steps,),
        in_specs=[row(D_MODEL), _const_spec((1, D_MODEL)), _const_spec((D_MODEL, D_FF)),
                  _const_spec((D_MODEL, D_FF)), _const_spec((D_FF, D_MODEL))] + cast_specs,
        out_specs=[row(D_MODEL)] + cast_specs,
        out_shape=[jax.ShapeDtypeStruct((m, D_MODEL), F32)]
                  + [jax.ShapeDtypeStruct(w.shape, BF16) for w in to_cast],
        scratch_shapes=[pltpu.VMEM((tm, D_FF), BF16)],
        compiler_params=pltpu.CompilerParams(dimension_semantics=("arbitrary",),
                                             vmem_limit_bytes=VMEM_LIMIT),
        name="ffn1",
    )(x, n1, wg, wu, wd, *to_cast)
    return outs[0], outs[1:]


def _proj_body(first_of_seq, x1_ref, nm_ref, win_ref, wvt_ref, vones_ref, cw_ref, cb_ref, dtb_ref,
               a_ref, xs_ref, bc_ref, z_ref, q_ref, k_ref, vt_ref, dtc_ref, acsc_ref, acst_ref, buf_ref):
    tm = T_SSD

    @pl.when(first_of_seq)
    def _():
        buf_ref[0:SUBLANES, :] = jnp.zeros((SUBLANES, CONV_DIM), F32)

    @pl.when(jnp.logical_not(first_of_seq))
    def _():
        buf_ref[0:SUBLANES, :] = buf_ref[tm:tm + SUBLANES, :]

    h2 = _rms(x1_ref[...], nm_ref[...]).astype(BF16)

    o = PROJ_PAD - DT_PAD
    dtc_ref[...] = jnp.dot(h2, win_ref[:, o:o + DT_PAD], preferred_element_type=F32)
    xr = dtc_ref[...].T + dtb_ref[...]
    dt_t = jnp.maximum(xr, 0.0) + jnp.log1p(jnp.exp(-jnp.abs(xr)))
    acs_t = dt_t * a_ref[...]
    pos = lax.broadcasted_iota(jnp.int32, (DT_PAD, tm), 1)
    shift = 1
    while shift < tm:
        acs_t = acs_t + jnp.where(pos >= shift, pltpu.roll(acs_t, shift, 1), 0.0)
        shift *= 2
    acst_ref[...] = acs_t
    dtc_ref[...] = dt_t.T
    acsc_ref[...] = acs_t.T

    buf_ref[SUBLANES:SUBLANES + tm, :] = jnp.dot(h2, win_ref[:, 0:CONV_DIM], preferred_element_type=F32)
    vt = lax.dot_general(wvt_ref[...], h2, (((1,), (1,)), ((), ())), preferred_element_type=F32)
    vt_ref[...] = (vt + vones_ref[...]).astype(BF16)
    o = CONV_DIM
    for ref in (z_ref, q_ref, k_ref):
        n = ref.shape[-1]
        ref[...] = jnp.dot(h2, win_ref[:, o:o + n], preferred_element_type=F32).astype(ref.dtype)
        o += n

    acc = cb_ref[...]
    for k in range(CONV_WIDTH):
        off = SUBLANES - (CONV_WIDTH - 1) + k
        acc = acc + buf_ref[off:off + tm, :] * cw_ref[k:k + 1, :]
    xbc = _silu(acc)
    xs_ref[...] = xbc[:, :SSD_INNER]
    bc_ref[...] = xbc[:, SSD_INNER:].astype(BF16)


def _const_spec(shape):
    return pl.BlockSpec(shape, lambda *_: (0,) * len(shape), pipeline_mode=pl.Buffered(1))


def _ssd_body(xs_ref, bc_ref, dtc_ref, acsc_ref, acst_ref, z_ref, dsk_ref, nw_ref, y_ref, h_ref):
    t = T_SSD
    hb = t // 2

    @pl.when(pl.program_id(1) == 0)
    def _():
        h_ref[...] = jnp.zeros_like(h_ref)

    gn = SSD_GROUPS * SSD_STATE
    xs = xs_ref[0]
    bm_bf = bc_ref[0, :, 0:gn]
    cm_bf = bc_ref[0, :, gn:2 * gn]
    dt_c = dtc_ref[0]
    acs_c = acsc_ref[0]
    acs_t = acst_ref[...]

    causal = (lax.broadcasted_iota(jnp.int32, (hb, hb), 0) >= lax.broadcasted_iota(jnp.int32, (hb, hb), 1))
    first = lambda rows: lax.broadcasted_iota(jnp.int32, (rows, 2 * SSD_HEAD_DIM), 1) < SSD_HEAD_DIM
    lo, lo_st, lo_row = first(t), first(SSD_STATE), first(1)
    pw = 2 * SSD_HEAD_DIM
    ys = []
    for g in range(SSD_GROUPS):
        n0 = g * SSD_STATE
        cg = cm_bf[:, n0:n0 + SSD_STATE]
        cbm = lax.dot_general(cg, bm_bf[:, n0:n0 + SSD_STATE], (((1,), (1,)), ((), ())),
                              preferred_element_type=F32)
        bt = bm_bf[:, n0:n0 + SSD_STATE].astype(F32).T
        hprev = h_ref[g]
        yoff = jnp.dot(cg, hprev.astype(BF16), preferred_element_type=F32)
        for pr in range(HEADS_PER_GROUP // 2):
            h0 = g * HEADS_PER_GROUP + 2 * pr
            h1 = h0 + 1
            l0 = pr * pw
            x_pair = xs[:, h0 * SSD_HEAD_DIM:h0 * SSD_HEAD_DIM + pw]
            xdt = (x_pair * jnp.where(lo, dt_c[:, h0:h0 + 1], dt_c[:, h1:h1 + 1])).astype(BF16)
            lhs = []
            for hd in (h0, h1):
                col = acs_c[:, hd:hd + 1]
                row = acs_t[hd:hd + 1, :]
                d00 = jnp.exp2(jnp.where(causal, col[:hb] - row[:, :hb], -jnp.inf))
                d10 = jnp.exp2(col[hb:] - row[:, :hb])
                d11 = jnp.exp2(jnp.where(causal, col[hb:] - row[:, hb:], -jnp.inf))
                top = jnp.concatenate([(cbm[:hb, :hb] * d00).astype(BF16), jnp.zeros((hb, hb), BF16)], axis=1)
                bot = jnp.concatenate([(cbm[hb:, :hb] * d10).astype(BF16),
                                       (cbm[hb:, hb:] * d11).astype(BF16)], axis=1)
                lhs += [top, bot]
            lasts = [acs_t[hd:hd + 1, t - 1:t] for hd in (h0, h1)]
            for hd, last in zip((h0, h1), lasts):
                lhs.append((bt * jnp.exp2(last - acs_t[hd:hd + 1, :])).astype(BF16))
            r = jnp.dot(jnp.concatenate(lhs, axis=0), xdt, preferred_element_type=F32)
            yd = jnp.where(lo, r[0:t], r[t:2 * t])
            st = jnp.where(lo_st, r[2 * t:2 * t + SSD_STATE], r[2 * t + SSD_STATE:])
            ecol = jnp.exp2(jnp.where(lo, acs_c[:, h0:h0 + 1], acs_c[:, h1:h1 + 1]))
            ys.append(yd + yoff[:, l0:l0 + pw] * ecol)
            elast = jnp.exp2(jnp.where(lo_row, lasts[0], lasts[1]))
            h_ref[g, :, l0:l0 + pw] = hprev[:, l0:l0 + pw] * elast + st

    y = jnp.concatenate(ys, axis=1) + xs * dsk_ref[...]
    y = y * _silu(z_ref[0])
    y_ref[0] = _rms(y, nw_ref[...]).astype(y_ref.dtype)


def _attn_body(q_ref, k0_ref, k1_ref, k2_ref, vt0_ref, vt1_ref, vt2_ref, ring_ref, o_ref, st_ref,
               bias_ref):
    tq = TQ_ATT
    nk = KV_BLOCKS * tq
    lt = pl.program_id(1)

    @pl.when((pl.program_id(0) == 0) & (lt == 0))
    def _():
        kpos = lax.broadcasted_iota(jnp.int32, (nk, tq), 0)
        qpos = lax.broadcasted_iota(jnp.int32, (nk, tq), 1) + (KV_BLOCKS - 1) * tq
        dchunk = qpos // CHUNK - kpos // CHUNK
        neg = jnp.finfo(F32).min
        for hd in range(ATT_HEADS):
            ring = jnp.broadcast_to(ring_ref[hd:hd + 1, :], (nk, ATT_RING))
            toep = pltpu.roll(ring, 0, 1, stride=1, stride_axis=0)[:, 0:tq]
            band = jnp.where(dchunk >= 0, jnp.where(dchunk <= LEFT_CHUNKS, toep, neg), neg)
            for n in range(KV_BLOCKS):
                bias_ref[n, hd] = jnp.where(kpos >= (KV_BLOCKS - 1 - n) * tq, band, neg)

    case = jnp.minimum(lt, KV_BLOCKS - 1)
    pw = 2 * ATT_HEAD_DIM
    lo = lax.broadcasted_iota(jnp.int32, (tq, pw), 1) < ATT_HEAD_DIM
    nt = (((1,), (1,)), ((), ()))
    for hp in range(ATT_HEADS // 2):
        c0 = hp * pw
        qp = q_ref[0, :, c0:c0 + pw]
        zero = jnp.zeros_like(qp)
        qs = jnp.concatenate([jnp.where(lo, qp, zero), jnp.where(lo, zero, qp)], axis=0)
        for j, kr in enumerate((k0_ref, k1_ref, k2_ref)):
            st_ref[hp, j] = lax.dot_general(kr[0, :, c0:c0 + pw], qs, nt, preferred_element_type=F32)
    for hp in range(ATT_HEADS // 2):
        c0 = hp * pw
        outs = []
        for e in range(2):
            hd = 2 * hp + e
            m = None
            ot = None
            for j, vr in reversed(list(enumerate((vt0_ref, vt1_ref, vt2_ref)))):
                def scores(r0, r1, q0, q1):
                    return (st_ref[hp, j, r0:r1, e * tq + q0:e * tq + q1]
                            + bias_ref[case, hd, j * tq + r0:j * tq + r1, q0:q1])
                if j not in DEAD_QUADRANT:
                    s = scores(0, tq, 0, tq)
                    mj = jnp.max(s, axis=0, keepdims=True)
                    m_new = mj if m is None else jnp.maximum(m, mj)
                    p = jnp.exp((s - m_new).astype(BF16))
                else:
                    rh, qh = DEAD_QUADRANT[j]
                    hk, hq = tq // 2, tq // 2
                    dead_q = (KV_BLOCKS - 1) * tq + qh * hq + np.arange(hq)
                    dead_k = j * tq + rh * hk + np.arange(hk)
                    dist = dead_q[None, :] // CHUNK - dead_k[:, None] // CHUNK
                    assert not np.any((dist >= 0) & (dist <= LEFT_CHUNKS)), "quadrant is inside the band"
                    full = scores((1 - rh) * hk, (2 - rh) * hk, 0, tq)
                    part = scores(rh * hk, (rh + 1) * hk, (1 - qh) * hq, (2 - qh) * hq)
                    mf = jnp.max(full, axis=0, keepdims=True)
                    mp = jnp.max(part, axis=0, keepdims=True)
                    neg = jnp.full((1, hq), jnp.finfo(F32).min, F32)
                    mp = jnp.concatenate([neg, mp] if qh == 0 else [mp, neg], axis=1)
                    mj = jnp.maximum(mf, mp)
                    m_new = mj if m is None else jnp.maximum(m, mj)
                    pf = jnp.exp((full - m_new).astype(BF16))
                    pp = jnp.exp((part - m_new[:, (1 - qh) * hq:(2 - qh) * hq]).astype(BF16))
                    zeros = jnp.zeros((hk, hq), BF16)
                    pp = jnp.concatenate([zeros, pp] if qh == 0 else [pp, zeros], axis=1)
                    p = jnp.concatenate([pp, pf] if rh == 0 else [pf, pp], axis=0)
                oj = jnp.dot(vr[hd * VT_HEAD_ROWS:(hd + 1) * VT_HEAD_ROWS, :], p,
                             preferred_element_type=F32)
                ot = oj if ot is None else ot * jnp.exp(m - m_new) + oj
                m = m_new
            outs.append(ot[0:ATT_HEAD_DIM] / ot[ATT_HEAD_DIM:ATT_HEAD_DIM + 1])
        o_ref[0, :, c0:c0 + pw] = jnp.concatenate(outs, axis=0).T.astype(o_ref.dtype)


def _attn_bias_ring(rel_bias):
    q0 = (KV_BLOCKS - 1) * TQ_ATT
    j = np.arange(ATT_RING)
    e = np.where(j <= TQ_ATT, -j, ATT_RING - j)
    idx = np.clip(q0 - e, -MAX_REL, MAX_REL) + MAX_REL
    lo, hi = q0 - MAX_REL + 1, ATT_RING - (q0 - MAX_REL)
    assert np.all(idx[:lo] == 2 * MAX_REL) and np.all(idx[hi:] == 2 * MAX_REL)
    assert np.all(idx[lo:hi] == np.arange(1, hi - lo + 1))
    rb = rel_bias.astype(F32)
    far = rb[:, 2 * MAX_REL:]
    return jnp.concatenate([jnp.repeat(far, lo, axis=1), rb[:, 1:hi - lo + 1],
                            jnp.repeat(far, ATT_RING - hi, axis=1)], axis=1)


def _front_kernel(x1_ref, nm_ref, win_ref, wvt_ref, vones_ref, cw_ref, cb_ref, dtb_ref, a_ref, dsk_ref,
                  nw_ref, ring_ref, y_ref, o_ref, buf_ref, h_ref, st_ref, bias_ref, kring_ref, vtring_ref,
                  xs_ref, bc_ref, z_ref, q_ref, dtc_ref, acsc_ref, acst_ref):
    lt = pl.program_id(1)

    @pl.when((pl.program_id(0) == 0) & (lt == 0))
    def _():
        kring_ref[...] = jnp.zeros_like(kring_ref)
        vtring_ref[...] = jnp.zeros_like(vtring_ref)

    slot = lambda back: (lt + (KV_BLOCKS - back)) % KV_BLOCKS
    _proj_body(lt == 0, x1_ref.at[0], nm_ref, win_ref, wvt_ref, vones_ref, cw_ref, cb_ref, dtb_ref, a_ref,
               xs_ref.at[0], bc_ref.at[0], z_ref.at[0], q_ref.at[0], kring_ref.at[slot(0)],
               vtring_ref.at[slot(0)], dtc_ref.at[0], acsc_ref.at[0], acst_ref, buf_ref)
    keys = [kring_ref.at[pl.ds(slot(back), 1)] for back in (2, 1, 0)]
    vts = [vtring_ref.at[slot(back)] for back in (2, 1, 0)]
    _attn_body(q_ref, *keys, *vts, ring_ref, o_ref, st_ref, bias_ref)
    _ssd_body(xs_ref, bc_ref, dtc_ref, acsc_ref, acst_ref, z_ref, dsk_ref, nw_ref, y_ref, h_ref)


def _front(x1, nm, win, wvt, vones, cw, cb, dtb, a, dsk, nw, ring):
    assert T_SSD == TQ_ATT
    bsz, seqlen, _ = x1.shape
    t = T_SSD
    tok = lambda n: pl.BlockSpec((1, t, n), lambda b, i: (b, i, 0))
    bc_dim = CONV_DIM - SSD_INNER
    return pl.pallas_call(
        _front_kernel,
        grid=(bsz, seqlen // t),
        in_specs=[tok(D_MODEL), _const_spec((1, D_MODEL)), _const_spec((D_MODEL, PROJ_PAD)),
                  _const_spec((VT_ROWS, D_MODEL)), _const_spec((VT_ROWS, 1)),
                  _const_spec((CONV_WIDTH, CONV_DIM)), _const_spec((1, CONV_DIM)),
                  _const_spec((DT_PAD, 1)), _const_spec((DT_PAD, 1)), _const_spec((1, SSD_INNER)),
                  _const_spec((1, SSD_INNER)), _const_spec((ATT_HEADS, ATT_RING))],
        out_specs=[tok(SSD_INNER), tok(ATT_INNER)],
        out_shape=[jax.ShapeDtypeStruct((bsz, seqlen, SSD_INNER), BF16),
                   jax.ShapeDtypeStruct((bsz, seqlen, ATT_INNER), BF16)],
        scratch_shapes=[pltpu.VMEM((t + SUBLANES, CONV_DIM), F32),
                        pltpu.VMEM((SSD_GROUPS, SSD_STATE, HEADS_PER_GROUP * SSD_HEAD_DIM), F32),
                        pltpu.VMEM((ATT_HEADS // 2, KV_BLOCKS, t, 2 * t), F32),
                        pltpu.VMEM((KV_BLOCKS, ATT_HEADS, KV_BLOCKS * t, t), F32),
                        pltpu.VMEM((KV_BLOCKS, t, ATT_INNER), BF16),
                        pltpu.VMEM((KV_BLOCKS, VT_ROWS, t), BF16),
                        pltpu.VMEM((1, t, SSD_INNER), F32), pltpu.VMEM((1, t, bc_dim), BF16),
                        pltpu.VMEM((1, t, SSD_INNER), F32), pltpu.VMEM((1, t, ATT_INNER), BF16),
                        pltpu.VMEM((1, t, DT_PAD), F32), pltpu.VMEM((1, t, DT_PAD), F32),
                        pltpu.VMEM((DT_PAD, t), F32)],
        compiler_params=pltpu.CompilerParams(dimension_semantics=("arbitrary", "arbitrary"),
                                             vmem_limit_bytes=VMEM_LIMIT),
        name="front",
    )(x1, nm, win, wvt, vones, cw, cb, dtb, a, dsk, nw, ring)


def _out_ffn2_kernel(x1_ref, y_ref, att_ref, wo_ref, n2_ref, wg_ref, wu_ref, wd_ref, nf_ref,
                     out_ref, act_ref):
    mix = jnp.dot(y_ref[...], wo_ref[0:SSD_INNER, :], preferred_element_type=F32)
    mix = mix + jnp.dot(att_ref[...], wo_ref[SSD_INNER:MIX_WIDTH, :], preferred_element_type=F32)
    x2 = x1_ref[...] + mix
    h = _rms(x2, n2_ref[...]).astype(BF16)
    x3 = x2 + 0.5 * _swiglu(h, wg_ref, wu_ref, wd_ref, act_ref)
    out_ref[...] = _rms(x3, nf_ref[...])


def _out_ffn2(x1, y, att, wo, n2, wg, wu, wd, nf):
    m = x1.shape[0]
    tm = TM_FFN2
    row = lambda n: pl.BlockSpec((tm, n), lambda i: (i, 0))
    return pl.pallas_call(
        _out_ffn2_kernel,
        grid=(m // tm,),
        in_specs=[row(D_MODEL), row(SSD_INNER), row(ATT_INNER), _const_spec((MIX_WIDTH, D_MODEL)),
                  _const_spec((1, D_MODEL)), _const_spec((D_MODEL, D_FF)), _const_spec((D_MODEL, D_FF)),
                  _const_spec((D_FF, D_MODEL)), _const_spec((1, D_MODEL))],
        out_specs=row(D_MODEL),
        out_shape=jax.ShapeDtypeStruct((m, D_MODEL), F32),
        scratch_shapes=[pltpu.VMEM((tm, D_FF), BF16)],
        compiler_params=pltpu.CompilerParams(dimension_semantics=("parallel",),
                                             vmem_limit_bytes=VMEM_LIMIT),
        name="out_ffn2",
    )(x1, y, att, wo, n2, wg, wu, wd, nf)


def _layer(x, ffn1_norm, ffn1_w_gate, ffn1_w_up, ffn1_w_down, mix_norm, w_in, conv_w, conv_b, dt_bias,
           a_log, d_skip, ssd_norm, rel_bias, w_out, ffn2_norm, ffn2_w_gate, ffn2_w_up, ffn2_w_down,
           out_norm):
    bsz, seqlen, _ = x.shape
    m = bsz * seqlen
    o_xbc = SSD_INNER
    o_dt = o_xbc + CONV_DIM
    o_q = o_dt + SSD_HEADS
    o_k = o_q + ATT_INNER
    o_v = o_k + ATT_INNER
    row = lambda p: p.reshape(1, -1).astype(F32)
    pad_col = lambda p: jnp.pad(p.astype(F32), (0, DT_PAD - SSD_HEADS)).reshape(DT_PAD, 1)

    x1, (wt_bf, wo_bf, wg2_bf, wu2_bf, wd2_bf) = _ffn1(
        x.reshape(m, D_MODEL), row(ffn1_norm), ffn1_w_gate.astype(BF16), ffn1_w_up.astype(BF16),
        ffn1_w_down.astype(BF16), [w_in.T, w_out, ffn2_w_gate, ffn2_w_up, ffn2_w_down])

    win = jnp.concatenate(
        [wt_bf[o_xbc:o_dt], wt_bf[:o_xbc], wt_bf[o_q:o_k] * ATT_HEAD_DIM ** -0.5, wt_bf[o_k:o_v],
         wt_bf[o_dt:o_q], jnp.zeros((DT_PAD - SSD_HEADS, D_MODEL), BF16)], axis=0).T
    pad_rows = ((0, 0), (0, VT_HEAD_ROWS - ATT_HEAD_DIM), (0, 0))
    wvt = jnp.pad(wt_bf[o_v:].reshape(ATT_HEADS, ATT_HEAD_DIM, D_MODEL), pad_rows)
    wvt = wvt.reshape(VT_ROWS, D_MODEL)
    vones = jnp.zeros((ATT_HEADS, VT_HEAD_ROWS), F32).at[:, ATT_HEAD_DIM].set(1.0).reshape(VT_ROWS, 1)

    a_log2 = -jnp.exp(a_log.astype(F32)) * math.log2(math.e)
    y, att = _front(x1.reshape(bsz, seqlen, D_MODEL), row(mix_norm), win, wvt, vones, conv_w.astype(F32),
                    row(conv_b), pad_col(dt_bias), pad_col(a_log2), row(jnp.repeat(d_skip, SSD_HEAD_DIM)),
                    row(ssd_norm), _attn_bias_ring(rel_bias))

    out = _out_ffn2(x1, y.reshape(m, SSD_INNER), att.reshape(m, ATT_INNER), wo_bf, row(ffn2_norm), wg2_bf,
                    wu2_bf, wd2_bf, row(out_norm))
    return out.reshape(bsz, seqlen, D_MODEL)


def kernel(x, ffn1_norm, ffn1_w_gate, ffn1_w_up, ffn1_w_down, mix_norm, w_in, conv_w, conv_b, dt_bias, a_log, d_skip, ssd_norm, rel_bias, w_out, ffn2_norm, ffn2_w_gate, ffn2_w_up, ffn2_w_down, final_norm):
    depth = ffn1_norm.shape[0]
    assert depth == 1, "the fused out_ffn2 kernel applies the final norm after the only layer"
    return _layer(x, ffn1_norm[0], ffn1_w_gate[0], ffn1_w_up[0], ffn1_w_down[0], mix_norm[0], w_in[0],
                  conv_w[0], conv_b[0], dt_bias[0], a_log[0], d_skip[0], ssd_norm[0], rel_bias[0],
                  w_out[0], ffn2_norm[0], ffn2_w_gate[0], ffn2_w_up[0], ffn2_w_down[0], final_norm)
```

```python
import functools
import math

import jax
import jax.numpy as jnp
import numpy as np
from jax import lax
from jax.experimental import pallas as pl
from jax.experimental.pallas import tpu as pltpu

D_MODEL = 1024
CHUNK = 64
LEFT_CHUNKS = 8
EPS = 1e-5
D_FF = 2816
SSD_HEADS = 16
SSD_HEAD_DIM = 64
SSD_INNER = SSD_HEADS * SSD_HEAD_DIM
SSD_GROUPS = 2
HEADS_PER_GROUP = SSD_HEADS // SSD_GROUPS
SSD_STATE = 128
CONV_WIDTH = 4
CONV_DIM = SSD_INNER + 2 * SSD_GROUPS * SSD_STATE
ATT_HEADS = 8
ATT_HEAD_DIM = 64
ATT_INNER = ATT_HEADS * ATT_HEAD_DIM
MAX_REL = 256
MIX_WIDTH = SSD_INNER + ATT_INNER

LANES = 128
SUBLANES = 8
DT_PAD = LANES
PROJ_PAD = CONV_DIM + SSD_INNER + 2 * ATT_INNER + DT_PAD
BF16_SUBLANES = 16
VT_HEAD_ROWS = ATT_HEAD_DIM + BF16_SUBLANES
VT_ROWS = ATT_HEADS * VT_HEAD_ROWS

TM_FFN1 = 512
TM_FFN2 = 512
T_SSD = 256
TQ_ATT = 256
KV_BLOCKS = 3
ATT_RING = (KV_BLOCKS + 1) * TQ_ATT
DEAD_QUADRANT = {0: (0, 1), KV_BLOCKS - 1: (1, 0)}
F_CHUNKS = ((0, 1024), (1024, 2048), (2048, D_FF))
VMEM_LIMIT = 56 * 1024 * 1024

F32 = jnp.float32
BF16 = jnp.bfloat16


def _rms(x, w):
    return x * lax.rsqrt(jnp.mean(x * x, axis=-1, keepdims=True) + EPS) * w


def _silu(x):
    return x * (1.0 / (1.0 + jnp.exp(-x)))


def _swiglu(h, wg_ref, wu_ref, wd_ref, act_ref):
    for lo, hi in F_CHUNKS:
        g = jnp.dot(h, wg_ref[:, lo:hi], preferred_element_type=F32)
        u = jnp.dot(h, wu_ref[:, lo:hi], preferred_element_type=F32)
        act_ref[:, lo:hi] = (_silu(g) * u).astype(BF16)
    return jnp.dot(act_ref[...], wd_ref[...], preferred_element_type=F32)


def _ffn1_kernel(n_cast, x_ref, n1_ref, wg_ref, wu_ref, wd_ref, *refs):
    cast_in, x1_ref, cast_out, act_ref = refs[:n_cast], refs[n_cast], refs[n_cast + 1:-1], refs[-1]
    x = x_ref[...]
    h = _rms(x, n1_ref[...]).astype(BF16)
    x1_ref[...] = x + 0.5 * _swiglu(h, wg_ref, wu_ref, wd_ref, act_ref)
    for src, dst in zip(cast_in, cast_out):
        dst[...] = src[...].astype(BF16)


def _cast_block_rows(rows, steps):
    return -(-rows // (steps * BF16_SUBLANES)) * BF16_SUBLANES


def _ffn1(x, n1, wg, wu, wd, to_cast):
    m = x.shape[0]
    tm = TM_FFN1
    steps = m // tm
    row = lambda n: pl.BlockSpec((tm, n), lambda i: (i, 0))

    def cast_spec(w):
        rb = _cast_block_rows(w.shape[0], steps)
        last = -(-w.shape[0] // rb) - 1
        return pl.BlockSpec((rb, w.shape[1]), lambda i: (jnp.minimum(i, last), 0))

    cast_specs = [cast_spec(w) for w in to_cast]
    outs = pl.pallas_call(
        functools.partial(_ffn1_kernel, len(to_cast)),
        grid=(steps,),
        in_specs=[row(D_MODEL), _const_spec((1, D_MODEL)), _const_spec((D_MODEL, D_FF)),
                  _const_spec((D_MODEL, D_FF)), _const_spec((D_FF, D_MODEL))] + cast_specs,
        out_specs=[row(D_MODEL)] + cast_specs,
        out_shape=[jax.ShapeDtypeStruct((m, D_MODEL), F32)]
                  + [jax.ShapeDtypeStruct(w.shape, BF16) for w in to_cast],
        scratch_shapes=[pltpu.VMEM((tm, D_FF), BF16)],
        compiler_params=pltpu.CompilerParams(dimension_semantics=("arbitrary",),
                                             vmem_limit_bytes=VMEM_LIMIT),
        name="ffn1",
    )(x, n1, wg, wu, wd, *to_cast)
    return outs[0], outs[1:]


def _proj_body(first_of_seq, x1_ref, nm_ref, win_ref, wvt_ref, vones_ref, cw_ref, cb_ref, dtb_ref,
               a_ref, xs_ref, bc_ref, z_ref, q_ref, k_ref, vt_ref, dtc_ref, acsc_ref, acst_ref, buf_ref):
    tm = T_SSD

    @pl.when(first_of_seq)
    def _():
        buf_ref[0:SUBLANES, :] = jnp.zeros((SUBLANES, CONV_DIM), F32)

    @pl.when(jnp.logical_not(first_of_seq))
    def _():
        buf_ref[0:SUBLANES, :] = buf_ref[tm:tm + SUBLANES, :]

    h2 = _rms(x1_ref[...], nm_ref[...]).astype(BF16)

    o = PROJ_PAD - DT_PAD
    dtc_ref[...] = jnp.dot(h2, win_ref[:, o:o + DT_PAD], preferred_element_type=F32)
    xr = dtc_ref[...].T + dtb_ref[...]
    dt_t = jnp.maximum(xr, 0.0) + jnp.log1p(jnp.exp(-jnp.abs(xr)))
    acs_t = dt_t * a_ref[...]
    pos = lax.broadcasted_iota(jnp.int32, (DT_PAD, tm), 1)
    shift = 1
    while shift < tm:
        acs_t = acs_t + jnp.where(pos >= shift, pltpu.roll(acs_t, shift, 1), 0.0)
        shift *= 2
    acst_ref[...] = acs_t
    dtc_ref[...] = dt_t.T
    acsc_ref[...] = acs_t.T

    buf_ref[SUBLANES:SUBLANES + tm, :] = jnp.dot(h2, win_ref[:, 0:CONV_DIM], preferred_element_type=F32)
    vt = lax.dot_general(wvt_ref[...], h2, (((1,), (1,)), ((), ())), preferred_element_type=F32)
    vt_ref[...] = (vt + vones_ref[...]).astype(BF16)
    o = CONV_DIM
    for ref in (z_ref, q_ref, k_ref):
        n = ref.shape[-1]
        ref[...] = jnp.dot(h2, win_ref[:, o:o + n], preferred_element_type=F32).astype(ref.dtype)
        o += n

    acc = cb_ref[...]
    for k in range(CONV_WIDTH):
        off = SUBLANES - (CONV_WIDTH - 1) + k
        acc = acc + buf_ref[off:off + tm, :] * cw_ref[k:k + 1, :]
    xbc = _silu(acc)
    xs_ref[...] = xbc[:, :SSD_INNER]
    bc_ref[...] = xbc[:, SSD_INNER:].astype(BF16)


def _const_spec(shape):
    return pl.BlockSpec(shape, lambda *_: (0,) * len(shape), pipeline_mode=pl.Buffered(1))


def _ssd_body(xs_ref, bc_ref, dtc_ref, acsc_ref, acst_ref, z_ref, dsk_ref, nw_ref, y_ref, h_ref):
    t = T_SSD
    hb = t // 2

    @pl.when(pl.program_id(1) == 0)
    def _():
        h_ref[...] = jnp.zeros_like(h_ref)

    gn = SSD_GROUPS * SSD_STATE
    xs = xs_ref[0]
    bm_bf = bc_ref[0, :, 0:gn]
    cm_bf = bc_ref[0, :, gn:2 * gn]
    dt_c = dtc_ref[0]
    acs_c = acsc_ref[0]
    acs_t = acst_ref[...]

    causal = (lax.broadcasted_iota(jnp.int32, (hb, hb), 0) >= lax.broadcasted_iota(jnp.int32, (hb, hb), 1))
    first = lambda rows: lax.broadcasted_iota(jnp.int32, (rows, 2 * SSD_HEAD_DIM), 1) < SSD_HEAD_DIM
    lo, lo_st, lo_row = first(t), first(SSD_STATE), first(1)
    pw = 2 * SSD_HEAD_DIM
    ys = []
    for g in range(SSD_GROUPS):
        n0 = g * SSD_STATE
        cg = cm_bf[:, n0:n0 + SSD_STATE]
        cbm = lax.dot_general(cg, bm_bf[:, n0:n0 + SSD_STATE], (((1,), (1,)), ((), ())),
                              preferred_element_type=F32)
        bt = bm_bf[:, n0:n0 + SSD_STATE].astype(F32).T
        hprev = h_ref[g]
        yoff = jnp.dot(cg, hprev.astype(BF16), preferred_element_type=F32)
        for pr in range(HEADS_PER_GROUP // 2):
            h0 = g * HEADS_PER_GROUP + 2 * pr
            h1 = h0 + 1
            l0 = pr * pw
            x_pair = xs[:, h0 * SSD_HEAD_DIM:h0 * SSD_HEAD_DIM + pw]
            xdt = (x_pair * jnp.where(lo, dt_c[:, h0:h0 + 1], dt_c[:, h1:h1 + 1])).astype(BF16)
            lhs = []
            for hd in (h0, h1):
                col = acs_c[:, hd:hd + 1]
                row = acs_t[hd:hd + 1, :]
                d00 = jnp.exp2(jnp.where(causal, col[:hb] - row[:, :hb], -jnp.inf))
                d10 = jnp.exp2(col[hb:] - row[:, :hb])
                d11 = jnp.exp2(jnp.where(causal, col[hb:] - row[:, hb:], -jnp.inf))
                top = jnp.concatenate([(cbm[:hb, :hb] * d00).astype(BF16), jnp.zeros((hb, hb), BF16)], axis=1)
                bot = jnp.concatenate([(cbm[hb:, :hb] * d10).astype(BF16),
                                       (cbm[hb:, hb:] * d11).astype(BF16)], axis=1)
                lhs += [top, bot]
            lasts = [acs_t[hd:hd + 1, t - 1:t] for hd in (h0, h1)]
            for hd, last in zip((h0, h1), lasts):
                lhs.append((bt * jnp.exp2(last - acs_t[hd:hd + 1, :])).astype(BF16))
            r = jnp.dot(jnp.concatenate(lhs, axis=0), xdt, preferred_element_type=F32)
            yd = jnp.where(lo, r[0:t], r[t:2 * t])
            st = jnp.where(lo_st, r[2 * t:2 * t + SSD_STATE], r[2 * t + SSD_STATE:])
            ecol = jnp.exp2(jnp.where(lo, acs_c[:, h0:h0 + 1], acs_c[:, h1:h1 + 1]))
            ys.append(yd + yoff[:, l0:l0 + pw] * ecol)
            elast = jnp.exp2(jnp.where(lo_row, lasts[0], lasts[1]))
            h_ref[g, :, l0:l0 + pw] = hprev[:, l0:l0 + pw] * elast + st

    y = jnp.concatenate(ys, axis=1) + xs * dsk_ref[...]
    y = y * _silu(z_ref[0])
    y_ref[0] = _rms(y, nw_ref[...]).astype(y_ref.dtype)


def _attn_body(q_ref, k0_ref, k1_ref, k2_ref, vt0_ref, vt1_ref, vt2_ref, ring_ref, o_ref, st_ref,
               bias_ref):
    tq = TQ_ATT
    nk = KV_BLOCKS * tq
    lt = pl.program_id(1)

    @pl.when((pl.program_id(0) == 0) & (lt == 0))
    def _():
        kpos = lax.broadcasted_iota(jnp.int32, (nk, tq), 0)
        qpos = lax.broadcasted_iota(jnp.int32, (nk, tq), 1) + (KV_BLOCKS - 1) * tq
        dchunk = qpos // CHUNK - kpos // CHUNK
        neg = jnp.finfo(F32).min
        for hd in range(ATT_HEADS):
            ring = jnp.broadcast_to(ring_ref[hd:hd + 1, :], (nk, ATT_RING))
            toep = pltpu.roll(ring, 0, 1, stride=1, stride_axis=0)[:, 0:tq]
            band = jnp.where(dchunk >= 0, jnp.where(dchunk <= LEFT_CHUNKS, toep, neg), neg)
            for n in range(KV_BLOCKS):
                bias_ref[n, hd] = jnp.where(kpos >= (KV_BLOCKS - 1 - n) * tq, band, neg)

    case = jnp.minimum(lt, KV_BLOCKS - 1)
    pw = 2 * ATT_HEAD_DIM
    lo = lax.broadcasted_iota(jnp.int32, (tq, pw), 1) < ATT_HEAD_DIM
    nt = (((1,), (1,)), ((), ()))
    hk, hq = tq // 2, tq // 2
    for hp in range(ATT_HEADS // 2):
        c0 = hp * pw
        qp = q_ref[0, :, c0:c0 + pw]
        zero = jnp.zeros_like(qp)
        qs = jnp.concatenate([jnp.where(lo, qp, zero), jnp.where(lo, zero, qp)], axis=0)
        for j, kr in enumerate((k0_ref, k1_ref, k2_ref)):
            st_ref[hp, j] = lax.dot_general(kr[0, :, c0:c0 + pw], qs, nt, preferred_element_type=F32)
    for hp in range(ATT_HEADS // 2):
        c0 = hp * pw
        outs = []
        for e in range(2):
            hd = 2 * hp + e
            m = None
            ot = None
            for j, vr in reversed(list(enumerate((vt0_ref, vt1_ref, vt2_ref)))):
                def scores(r0, r1, q0, q1):
                    return (st_ref[hp, j, r0:r1, e * tq + q0:e * tq + q1]
                            + bias_ref[case, hd, j * tq + r0:j * tq + r1, q0:q1])
                if j not in DEAD_QUADRANT:
                    s = scores(0, tq, 0, tq)
                    mj = jnp.max(s, axis=0, keepdims=True)
                    m_new = mj if m is None else jnp.maximum(m, mj)
                    p = jnp.exp((s - m_new).astype(BF16))
                else:
                    rh, qh = DEAD_QUADRANT[j]
                    dead_q = (KV_BLOCKS - 1) * tq + qh * hq + np.arange(hq)
                    dead_k = j * tq + rh * hk + np.arange(hk)
                    dist = dead_q[None, :] // CHUNK - dead_k[:, None] // CHUNK
                    assert not np.any((dist >= 0) & (dist <= LEFT_CHUNKS)), "quadrant is inside the band"
                    full = scores((1 - rh) * hk, (2 - rh) * hk, 0, tq)
                    part = scores(rh * hk, (rh + 1) * hk, (1 - qh) * hq, (2 - qh) * hq)
                    mf = jnp.max(full, axis=0, keepdims=True)
                    mp = jnp.max(part, axis=0, keepdims=True)
                    neg = jnp.full((1, hq), jnp.finfo(F32).min, F32)
                    mp = jnp.concatenate([neg, mp] if qh == 0 else [mp, neg], axis=1)
                    mj = jnp.maximum(mf, mp)
                    m_new = mj if m is None else jnp.maximum(m, mj)
                    pf = jnp.exp((full - m_new).astype(BF16))
                    pp = jnp.exp((part - m_new[:, (1 - qh) * hq:(2 - qh) * hq]).astype(BF16))
                    zeros = jnp.zeros((hk, hq), BF16)
                    pp = jnp.concatenate([zeros, pp] if qh == 0 else [pp, zeros], axis=1)
                    p = jnp.concatenate([pp, pf] if rh == 0 else [pf, pp], axis=0)
                oj = jnp.dot(vr[hd * VT_HEAD_ROWS:(hd + 1) * VT_HEAD_ROWS, :], p,
                             preferred_element_type=F32)
                ot = oj if ot is None else ot * jnp.exp(m - m_new) + oj
                m = m_new
            outs.append(ot[0:ATT_HEAD_DIM] / ot[ATT_HEAD_DIM:ATT_HEAD_DIM + 1])
        o_ref[0, :, c0:c0 + pw] = jnp.concatenate(outs, axis=0).T.astype(o_ref.dtype)


def _attn_bias_ring(rel_bias):
    q0 = (KV_BLOCKS - 1) * TQ_ATT
    j = np.arange(ATT_RING)
    e = np.where(j <= TQ_ATT, -j, ATT_RING - j)
    idx = np.clip(q0 - e, -MAX_REL, MAX_REL) + MAX_REL
    lo, hi = q0 - MAX_REL + 1, ATT_RING - (q0 - MAX_REL)
    assert np.all(idx[:lo] == 2 * MAX_REL) and np.all(idx[hi:] == 2 * MAX_REL)
    assert np.all(idx[lo:hi] == np.arange(1, hi - lo + 1))
    rb = rel_bias.astype(F32)
    far = rb[:, 2 * MAX_REL:]
    return jnp.concatenate([jnp.repeat(far, lo, axis=1), rb[:, 1:hi - lo + 1],
                            jnp.repeat(far, ATT_RING - hi, axis=1)], axis=1)


def _front_kernel(x1_ref, nm_ref, win_ref, wvt_ref, vones_ref, cw_ref, cb_ref, dtb_ref, a_ref, dsk_ref,
                  nw_ref, ring_ref, y_ref, o_ref, buf_ref, h_ref, st_ref, bias_ref, kring_ref, vtring_ref,
                  xs_ref, bc_ref, z_ref, q_ref, dtc_ref, acsc_ref, acst_ref):
    lt = pl.program_id(1)

    @pl.when((pl.program_id(0) == 0) & (lt == 0))
    def _():
        kring_ref[...] = jnp.zeros_like(kring_ref)
        vtring_ref[...] = jnp.zeros_like(vtring_ref)

    slot = lambda back: (lt + (KV_BLOCKS - back)) % KV_BLOCKS
    _proj_body(lt == 0, x1_ref.at[0], nm_ref, win_ref, wvt_ref, vones_ref, cw_ref, cb_ref, dtb_ref, a_ref,
               xs_ref.at[0], bc_ref.at[0], z_ref.at[0], q_ref.at[0], kring_ref.at[slot(0)],
               vtring_ref.at[slot(0)], dtc_ref.at[0], acsc_ref.at[0], acst_ref, buf_ref)
    keys = [kring_ref.at[pl.ds(slot(back), 1)] for back in (2, 1, 0)]
    vts = [vtring_ref.at[slot(back)] for back in (2, 1, 0)]
    _attn_body(q_ref, *keys, *vts, ring_ref, o_ref, st_ref, bias_ref)
    _ssd_body(xs_ref, bc_ref, dtc_ref, acsc_ref, acst_ref, z_ref, dsk_ref, nw_ref, y_ref, h_ref)


def _front(x1, nm, win, wvt, vones, cw, cb, dtb, a, dsk, nw, ring):
    assert T_SSD == TQ_ATT
    bsz, seqlen, _ = x1.shape
    t = T_SSD
    tok = lambda n: pl.BlockSpec((1, t, n), lambda b, i: (b, i, 0))
    bc_dim = CONV_DIM - SSD_INNER
    return pl.pallas_call(
        _front_kernel,
        grid=(bsz, seqlen // t),
        in_specs=[tok(D_MODEL), _const_spec((1, D_MODEL)), _const_spec((D_MODEL, PROJ_PAD)),
                  _const_spec((VT_ROWS, D_MODEL)), _const_spec((VT_ROWS, 1)),
                  _const_spec((CONV_WIDTH, CONV_DIM)), _const_spec((1, CONV_DIM)),
                  _const_spec((DT_PAD, 1)), _const_spec((DT_PAD, 1)), _const_spec((1, SSD_INNER)),
                  _const_spec((1, SSD_INNER)), _const_spec((ATT_HEADS, ATT_RING))],
        out_specs=[tok(SSD_INNER), tok(ATT_INNER)],
        out_shape=[jax.ShapeDtypeStruct((bsz, seqlen, SSD_INNER), BF16),
                   jax.ShapeDtypeStruct((bsz, seqlen, ATT_INNER), BF16)],
        scratch_shapes=[pltpu.VMEM((t + SUBLANES, CONV_DIM), F32),
                        pltpu.VMEM((SSD_GROUPS, SSD_STATE, HEADS_PER_GROUP * SSD_HEAD_DIM), F32),
                        pltpu.VMEM((ATT_HEADS // 2, KV_BLOCKS, t, 2 * t), F32),
                        pltpu.VMEM((KV_BLOCKS, ATT_HEADS, KV_BLOCKS * t, t), F32),
                        pltpu.VMEM((KV_BLOCKS, t, ATT_INNER), BF16),
                        pltpu.VMEM((KV_BLOCKS, VT_ROWS, t), BF16),
                        pltpu.VMEM((1, t, SSD_INNER), F32), pltpu.VMEM((1, t, bc_dim), BF16),
                        pltpu.VMEM((1, t, SSD_INNER), F32), pltpu.VMEM((1, t, ATT_INNER), BF16),
                        pltpu.VMEM((1, t, DT_PAD), F32), pltpu.VMEM((1, t, DT_PAD), F32),
                        pltpu.VMEM((DT_PAD, t), F32)],
        compiler_params=pltpu.CompilerParams(dimension_semantics=("arbitrary", "arbitrary"),
                                             vmem_limit_bytes=VMEM_LIMIT),
        name="front",
    )(x1, nm, win, wvt, vones, cw, cb, dtb, a, dsk, nw, ring)


def _out_ffn2_kernel(x1_ref, y_ref, att_ref, wo_ref, n2_ref, wg_ref, wu_ref, wd_ref, nf_ref,
                     out_ref, act_ref):
    mix = jnp.dot(y_ref[...], wo_ref[0:SSD_INNER, :], preferred_element_type=F32)
    mix = mix + jnp.dot(att_ref[...], wo_ref[SSD_INNER:MIX_WIDTH, :], preferred_element_type=F32)
    x2 = x1_ref[...] + mix
    h = _rms(x2, n2_ref[...]).astype(BF16)
    x3 = x2 + 0.5 * _swiglu(h, wg_ref, wu_ref, wd_ref, act_ref)
    out_ref[...] = _rms(x3, nf_ref[...])


def _out_ffn2(x1, y, att, wo, n2, wg, wu, wd, nf):
    m = x1.shape[0]
    tm = TM_FFN2
    row = lambda n: pl.BlockSpec((tm, n), lambda i: (i, 0))
    return pl.pallas_call(
        _out_ffn2_kernel,
        grid=(m // tm,),
        in_specs=[row(D_MODEL), row(SSD_INNER), row(ATT_INNER), _const_spec((MIX_WIDTH, D_MODEL)),
                  _const_spec((1, D_MODEL)), _const_spec((D_MODEL, D_FF)), _const_spec((D_MODEL, D_FF)),
                  _const_spec((D_FF, D_MODEL)), _const_spec((1, D_MODEL))],
        out_specs=row(D_MODEL),
        out_shape=jax.ShapeDtypeStruct((m, D_MODEL), F32),
        scratch_shapes=[pltpu.VMEM((tm, D_FF), BF16)],
        compiler_params=pltpu.CompilerParams(dimension_semantics=("parallel",),
                                             vmem_limit_bytes=VMEM_LIMIT),
        name="out_ffn2",
    )(x1, y, att, wo, n2, wg, wu, wd, nf)


def _layer(x, ffn1_norm, ffn1_w_gate, ffn1_w_up, ffn1_w_down, mix_norm, w_in, conv_w, conv_b, dt_bias,
           a_log, d_skip, ssd_norm, rel_bias, w_out, ffn2_norm, ffn2_w_gate, ffn2_w_up, ffn2_w_down,
           out_norm):
    bsz, seqlen, _ = x.shape
    m = bsz * seqlen
    o_xbc = SSD_INNER
    o_dt = o_xbc + CONV_DIM
    o_q = o_dt + SSD_HEADS
    o_k = o_q + ATT_INNER
    o_v = o_k + ATT_INNER
    row = lambda p: p.reshape(1, -1).astype(F32)
    pad_col = lambda p: jnp.pad(p.astype(F32), (0, DT_PAD - SSD_HEADS)).reshape(DT_PAD, 1)

    x1, (wt_bf, wo_bf, wg2_bf, wu2_bf, wd2_bf) = _ffn1(
        x.reshape(m, D_MODEL), row(ffn1_norm), ffn1_w_gate.astype(BF16), ffn1_w_up.astype(BF16),
        ffn1_w_down.astype(BF16), [w_in.T, w_out, ffn2_w_gate, ffn2_w_up, ffn2_w_down])

    win = jnp.concatenate(
        [wt_bf[o_xbc:o_dt], wt_bf[:o_xbc], wt_bf[o_q:o_k] * ATT_HEAD_DIM ** -0.5, wt_bf[o_k:o_v],
         wt_bf[o_dt:o_q], jnp.zeros((DT_PAD - SSD_HEADS, D_MODEL), BF16)], axis=0).T
    pad_rows = ((0, 0), (0, VT_HEAD_ROWS - ATT_HEAD_DIM), (0, 0))
    wvt = jnp.pad(wt_bf[o_v:].reshape(ATT_HEADS, ATT_HEAD_DIM, D_MODEL), pad_rows)
    wvt = wvt.reshape(VT_ROWS, D_MODEL)
    vones = jnp.zeros((ATT_HEADS, VT_HEAD_ROWS), F32).at[:, ATT_HEAD_DIM].set(1.0).reshape(VT_ROWS, 1)

    a_log2 = -jnp.exp(a_log.astype(F32)) * math.log2(math.e)
    y, att = _front(x1.reshape(bsz, seqlen, D_MODEL), row(mix_norm), win, wvt, vones, conv_w.astype(F32),
                    row(conv_b), pad_col(dt_bias), pad_col(a_log2), row(jnp.repeat(d_skip, SSD_HEAD_DIM)),
                    row(ssd_norm), _attn_bias_ring(rel_bias))

    out = _out_ffn2(x1, y.reshape(m, SSD_INNER), att.reshape(m, ATT_INNER), wo_bf, row(ffn2_norm), wg2_bf,
                    wu2_bf, wd2_bf, row(out_norm))
    return out.reshape(bsz, seqlen, D_MODEL)


def kernel(x, ffn1_norm, ffn1_w_gate, ffn1_w_up, ffn1_w_down, mix_norm, w_in, conv_w, conv_b, dt_bias, a_log, d_skip, ssd_norm, rel_bias, w_out, ffn2_norm, ffn2_w_gate, ffn2_w_up, ffn2_w_down, final_norm):
    depth = ffn1_norm.shape[0]
    assert depth == 1, "the fused out_ffn2 kernel applies the final norm after the only layer"
    return _layer(x, ffn1_norm[0], ffn1_w_gate[0], ffn1_w_up[0], ffn1_w_down[0], mix_norm[0], w_in[0],
                  conv_w[0], conv_b[0], dt_bias[0], a_log[0], d_skip[0], ssd_norm[0], rel_bias[0],
                  w_out[0], ffn2_norm[0], ffn2_w_gate[0], ffn2_w_up[0], ffn2_w_down[0], final_norm)
```

```python
import functools
import math

import jax
import jax.numpy as jnp
import numpy as np
from jax import lax
from jax.experimental import pallas as pl
from jax.experimental.pallas import tpu as pltpu

D_MODEL = 1024
CHUNK = 64
LEFT_CHUNKS = 8
EPS = 1e-5
D_FF = 2816
SSD_HEADS = 16
SSD_HEAD_DIM = 64
SSD_INNER = SSD_HEADS * SSD_HEAD_DIM
SSD_GROUPS = 2
HEADS_PER_GROUP = SSD_HEADS // SSD_GROUPS
SSD_STATE = 128
CONV_WIDTH = 4
CONV_DIM = SSD_INNER + 2 * SSD_GROUPS * SSD_STATE
ATT_HEADS = 8
ATT_HEAD_DIM = 64
ATT_INNER = ATT_HEADS * ATT_HEAD_DIM
MAX_REL = 256
MIX_WIDTH = SSD_INNER + ATT_INNER

LANES = 128
SUBLANES = 8
DT_PAD = LANES
PROJ_PAD = CONV_DIM + SSD_INNER + 2 * ATT_INNER + DT_PAD
BF16_SUBLANES = 16
VT_HEAD_ROWS = ATT_HEAD_DIM + BF16_SUBLANES
VT_ROWS = ATT_HEADS * VT_HEAD_ROWS

TM_FFN1 = 512
TM_FFN2 = 512
T_SSD = 256
TQ_ATT = 256
KV_BLOCKS = 3
ATT_RING = (KV_BLOCKS + 1) * TQ_ATT
DEAD_QUADRANT = {0: (0, 1), KV_BLOCKS - 1: (1, 0)}
F_CHUNKS = ((0, 1024), (1024, 2048), (2048, D_FF))
VMEM_LIMIT = 56 * 1024 * 1024

F32 = jnp.float32
BF16 = jnp.bfloat16


def _rms(x, w):
    return x * lax.rsqrt(jnp.mean(x * x, axis=-1, keepdims=True) + EPS) * w


def _silu(x):
    return x * (1.0 / (1.0 + jnp.exp(-x)))


def _swiglu(h, wg_ref, wu_ref, wd_ref, act_ref):
    for lo, hi in F_CHUNKS:
        g = jnp.dot(h, wg_ref[:, lo:hi], preferred_element_type=F32)
        u = jnp.dot(h, wu_ref[:, lo:hi], preferred_element_type=F32)
        act_ref[:, lo:hi] = (_silu(g) * u).astype(BF16)
    return jnp.dot(act_ref[...], wd_ref[...], preferred_element_type=F32)


def _ffn1_kernel(n_cast, x_ref, n1_ref, wg_ref, wu_ref, wd_ref, *refs):
    cast_in, x1_ref, cast_out, act_ref = refs[:n_cast], refs[n_cast], refs[n_cast + 1:-1], refs[-1]
    x = x_ref[...]
    h = _rms(x, n1_ref[...]).astype(BF16)
    x1_ref[...] = x + 0.5 * _swiglu(h, wg_ref, wu_ref, wd_ref, act_ref)
    for src, dst in zip(cast_in, cast_out):
        dst[...] = src[...].astype(BF16)


def _cast_block_rows(rows, steps):
    return -(-rows // (steps * BF16_SUBLANES)) * BF16_SUBLANES


def _ffn1(x, n1, wg, wu, wd, to_cast):
    m = x.shape[0]
    tm = TM_FFN1
    steps = m // tm
    row = lambda n: pl.BlockSpec((tm, n), lambda i: (i, 0))

    def cast_spec(w):
        rb = _cast_block_rows(w.shape[0], steps)
        last = -(-w.shape[0] // rb) - 1
        return pl.BlockSpec((rb, w.shape[1]), lambda i: (jnp.minimum(i, last), 0))

    cast_specs = [cast_spec(w) for w in to_cast]
    outs = pl.pallas_call(
        functools.partial(_ffn1_kernel, len(to_cast)),
        grid=(steps,),
        in_specs=[row(D_MODEL), _const_spec((1, D_MODEL)), _const_spec((D_MODEL, D_FF)),
                  _const_spec((D_MODEL, D_FF)), _const_spec((D_FF, D_MODEL))] + cast_specs,
        out_specs=[row(D_MODEL)] + cast_specs,
        out_shape=[jax.ShapeDtypeStruct((m, D_MODEL), F32)]
                  + [jax.ShapeDtypeStruct(w.shape, BF16) for w in to_cast],
        scratch_shapes=[pltpu.VMEM((tm, D_FF), BF16)],
        compiler_params=pltpu.CompilerParams(dimension_semantics=("arbitrary",),
                                             vmem_limit_bytes=VMEM_LIMIT),
        name="ffn1",
    )(x, n1, wg, wu, wd, *to_cast)
    return outs[0], outs[1:]


def _proj_body(first_of_seq, x1_ref, nm_ref, win_ref, wvt_ref, vones_ref, cw_ref, cb_ref, dtb_ref,
               a_ref, xs_ref, bc_ref, z_ref, q_ref, k_ref, vt_ref, dtc_ref, acsc_ref, acst_ref, buf_ref):
    tm = T_SSD

    @pl.when(first_of_seq)
    def _():
        buf_ref[0:SUBLANES, :] = jnp.zeros((SUBLANES, CONV_DIM), F32)

    @pl.when(jnp.logical_not(first_of_seq))
    def _():
        buf_ref[0:SUBLANES, :] = buf_ref[tm:tm + SUBLANES, :]

    h2 = _rms(x1_ref[...], nm_ref[...]).astype(BF16)

    o = PROJ_PAD - DT_PAD
    dtc_ref[...] = jnp.dot(h2, win_ref[:, o:o + DT_PAD], preferred_element_type=F32)
    xr = dtc_ref[...].T + dtb_ref[...]
    dt_t = jnp.maximum(xr, 0.0) + jnp.log1p(jnp.exp(-jnp.abs(xr)))
    acs_t = dt_t * a_ref[...]
    pos = lax.broadcasted_iota(jnp.int32, (DT_PAD, tm), 1)
    shift = 1
    while shift < tm:
        acs_t = acs_t + jnp.where(pos >= shift, pltpu.roll(acs_t, shift, 1), 0.0)
        shift *= 2
    acst_ref[...] = acs_t
    dtc_ref[...] = dt_t.T
    acsc_ref[...] = acs_t.T

    buf_ref[SUBLANES:SUBLANES + tm, :] = jnp.dot(h2, win_ref[:, 0:CONV_DIM], preferred_element_type=F32)
    vt = lax.dot_general(wvt_ref[...], h2, (((1,), (1,)), ((), ())), preferred_element_type=F32)
    vt_ref[...] = (vt + vones_ref[...]).astype(BF16)
    o = CONV_DIM
    for ref in (z_ref, q_ref, k_ref):
        n = ref.shape[-1]
        ref[...] = jnp.dot(h2, win_ref[:, o:o + n], preferred_element_type=F32).astype(ref.dtype)
        o += n

    acc = cb_ref[...]
    for k in range(CONV_WIDTH):
        off = SUBLANES - (CONV_WIDTH - 1) + k
        acc = acc + buf_ref[off:off + tm, :] * cw_ref[k:k + 1, :]
    xbc = _silu(acc)
    xs_ref[...] = xbc[:, :SSD_INNER]
    bc_ref[...] = xbc[:, SSD_INNER:].astype(BF16)


def _const_spec(shape):
    return pl.BlockSpec(shape, lambda *_: (0,) * len(shape), pipeline_mode=pl.Buffered(1))


def _ssd_body(xs_ref, bc_ref, dtc_ref, acsc_ref, acst_ref, z_ref, dsk_ref, nw_ref, y_ref, h_ref):
    t = T_SSD
    hb = t // 2

    @pl.when(pl.program_id(1) == 0)
    def _():
        h_ref[...] = jnp.zeros_like(h_ref)

    gn = SSD_GROUPS * SSD_STATE
    xs = xs_ref[0]
    bm_bf = bc_ref[0, :, 0:gn]
    cm_bf = bc_ref[0, :, gn:2 * gn]
    dt_c = dtc_ref[0]
    acs_c = acsc_ref[0]
    acs_t = acst_ref[...]

    causal = (lax.broadcasted_iota(jnp.int32, (hb, hb), 0) >= lax.broadcasted_iota(jnp.int32, (hb, hb), 1))
    first = lambda rows: lax.broadcasted_iota(jnp.int32, (rows, 2 * SSD_HEAD_DIM), 1) < SSD_HEAD_DIM
    lo, lo_st, lo_row = first(t), first(SSD_STATE), first(1)
    pw = 2 * SSD_HEAD_DIM
    ys = []
    for g in range(SSD_GROUPS):
        n0 = g * SSD_STATE
        cg = cm_bf[:, n0:n0 + SSD_STATE]
        cbm = lax.dot_general(cg, bm_bf[:, n0:n0 + SSD_STATE], (((1,), (1,)), ((), ())),
                              preferred_element_type=F32)
        cbm_bf = cbm.astype(BF16)
        bt = bm_bf[:, n0:n0 + SSD_STATE].astype(F32).T.astype(BF16)
        hprev = h_ref[g]
        yoff = jnp.dot(cg, hprev.astype(BF16), preferred_element_type=F32)
        for pr in range(HEADS_PER_GROUP // 2):
            h0 = g * HEADS_PER_GROUP + 2 * pr
            h1 = h0 + 1
            l0 = pr * pw
            x_pair = xs[:, h0 * SSD_HEAD_DIM:h0 * SSD_HEAD_DIM + pw]
            xdt = (x_pair * jnp.where(lo, dt_c[:, h0:h0 + 1], dt_c[:, h1:h1 + 1])).astype(BF16)
            lhs = []
            for hd in (h0, h1):
                col = acs_c[:, hd:hd + 1]
                row = acs_t[hd:hd + 1, :]
                d00 = jnp.exp2(jnp.where(causal, col[:hb] - row[:, :hb], -jnp.inf).astype(BF16))
                d10 = jnp.exp2((col[hb:] - row[:, :hb]).astype(BF16))
                d11 = jnp.exp2(jnp.where(causal, col[hb:] - row[:, hb:], -jnp.inf).astype(BF16))
                top = jnp.concatenate([cbm_bf[:hb, :hb] * d00, jnp.zeros((hb, hb), BF16)], axis=1)
                bot = jnp.concatenate([cbm_bf[hb:, :hb] * d10, cbm_bf[hb:, hb:] * d11], axis=1)
                lhs += [top, bot]
            lasts = [acs_t[hd:hd + 1, t - 1:t] for hd in (h0, h1)]
            for hd, last in zip((h0, h1), lasts):
                lhs.append(bt * jnp.exp2((last - acs_t[hd:hd + 1, :]).astype(BF16)))
            r = jnp.dot(jnp.concatenate(lhs, axis=0), xdt, preferred_element_type=F32)
            yd = jnp.where(lo, r[0:t], r[t:2 * t])
            st = jnp.where(lo_st, r[2 * t:2 * t + SSD_STATE], r[2 * t + SSD_STATE:])
            ecol = jnp.exp2(jnp.where(lo, acs_c[:, h0:h0 + 1], acs_c[:, h1:h1 + 1]))
            ys.append(yd + yoff[:, l0:l0 + pw] * ecol)
            elast = jnp.exp2(jnp.where(lo_row, lasts[0], lasts[1]))
            h_ref[g, :, l0:l0 + pw] = hprev[:, l0:l0 + pw] * elast + st

    y = jnp.concatenate(ys, axis=1) + xs * dsk_ref[...]
    y = y * _silu(z_ref[0])
    y_ref[0] = _rms(y, nw_ref[...]).astype(y_ref.dtype)


def _attn_body(q_ref, k0_ref, k1_ref, k2_ref, vt0_ref, vt1_ref, vt2_ref, ring_ref, o_ref, st_ref,
               bias_ref):
    tq = TQ_ATT
    nk = KV_BLOCKS * tq
    lt = pl.program_id(1)

    @pl.when((pl.program_id(0) == 0) & (lt == 0))
    def _():
        kpos = lax.broadcasted_iota(jnp.int32, (nk, tq), 0)
        qpos = lax.broadcasted_iota(jnp.int32, (nk, tq), 1) + (KV_BLOCKS - 1) * tq
        dchunk = qpos // CHUNK - kpos // CHUNK
        neg = jnp.finfo(F32).min
        for hd in range(ATT_HEADS):
            ring = jnp.broadcast_to(ring_ref[hd:hd + 1, :], (nk, ATT_RING))
            toep = pltpu.roll(ring, 0, 1, stride=1, stride_axis=0)[:, 0:tq]
            band = jnp.where(dchunk >= 0, jnp.where(dchunk <= LEFT_CHUNKS, toep, neg), neg)
            for n in range(KV_BLOCKS):
                bias_ref[n, hd] = jnp.where(kpos >= (KV_BLOCKS - 1 - n) * tq, band, neg)

    case = jnp.minimum(lt, KV_BLOCKS - 1)
    pw = 2 * ATT_HEAD_DIM
    lo = lax.broadcasted_iota(jnp.int32, (tq, pw), 1) < ATT_HEAD_DIM
    nt = (((1,), (1,)), ((), ()))
    hk, hq = tq // 2, tq // 2
    for hp in range(ATT_HEADS // 2):
        c0 = hp * pw
        qp = q_ref[0, :, c0:c0 + pw]
        zero = jnp.zeros_like(qp)
        qs = jnp.concatenate([jnp.where(lo, qp, zero), jnp.where(lo, zero, qp)], axis=0)
        for j, kr in enumerate((k0_ref, k1_ref, k2_ref)):
            st_ref[hp, j] = lax.dot_general(kr[0, :, c0:c0 + pw], qs, nt, preferred_element_type=F32)
    for hp in range(ATT_HEADS // 2):
        c0 = hp * pw
        outs = []
        for e in range(2):
            hd = 2 * hp + e
            m = None
            ot = None
            for j, vr in reversed(list(enumerate((vt0_ref, vt1_ref, vt2_ref)))):
                def scores(r0, r1, q0, q1):
                    return (st_ref[hp, j, r0:r1, e * tq + q0:e * tq + q1]
                            + bias_ref[case, hd, j * tq + r0:j * tq + r1, q0:q1])
                if j not in DEAD_QUADRANT:
                    s = scores(0, tq, 0, tq)
                    mj = jnp.max(s, axis=0, keepdims=True)
                    m_new = mj if m is None else jnp.maximum(m, mj)
                    p = jnp.exp((s - m_new).astype(BF16))
                else:
                    rh, qh = DEAD_QUADRANT[j]
                    dead_q = (KV_BLOCKS - 1) * tq + qh * hq + np.arange(hq)
                    dead_k = j * tq + rh * hk + np.arange(hk)
                    dist = dead_q[None, :] // CHUNK - dead_k[:, None] // CHUNK
                    assert not np.any((dist >= 0) & (dist <= LEFT_CHUNKS)), "quadrant is inside the band"
                    full = scores((1 - rh) * hk, (2 - rh) * hk, 0, tq)
                    part = scores(rh * hk, (rh + 1) * hk, (1 - qh) * hq, (2 - qh) * hq)
                    mf = jnp.max(full, axis=0, keepdims=True)
                    mp = jnp.max(part, axis=0, keepdims=True)
                    neg = jnp.full((1, hq), jnp.finfo(F32).min, F32)
                    mp = jnp.concatenate([neg, mp] if qh == 0 else [mp, neg], axis=1)
                    mj = jnp.maximum(mf, mp)
                    m_new = mj if m is None else jnp.maximum(m, mj)
                    pf = jnp.exp((full - m_new).astype(BF16))
                    pp = jnp.exp((part - m_new[:, (1 - qh) * hq:(2 - qh) * hq]).astype(BF16))
                    zeros = jnp.zeros((hk, hq), BF16)
                    pp = jnp.concatenate([zeros, pp] if qh == 0 else [pp, zeros], axis=1)
                    p = jnp.concatenate([pp, pf] if rh == 0 else [pf, pp], axis=0)
                oj = jnp.dot(vr[hd * VT_HEAD_ROWS:(hd + 1) * VT_HEAD_ROWS, :], p,
                             preferred_element_type=F32)
                ot = oj if ot is None else ot * jnp.exp(m - m_new) + oj
                m = m_new
            outs.append(ot[0:ATT_HEAD_DIM] / ot[ATT_HEAD_DIM:ATT_HEAD_DIM + 1])
        o_ref[0, :, c0:c0 + pw] = jnp.concatenate(outs, axis=0).T.astype(o_ref.dtype)


def _attn_bias_ring(rel_bias):
    q0 = (KV_BLOCKS - 1) * TQ_ATT
    j = np.arange(ATT_RING)
    e = np.where(j <= TQ_ATT, -j, ATT_RING - j)
    idx = np.clip(q0 - e, -MAX_REL, MAX_REL) + MAX_REL
    lo, hi = q0 - MAX_REL + 1, ATT_RING - (q0 - MAX_REL)
    assert np.all(idx[:lo] == 2 * MAX_REL) and np.all(idx[hi:] == 2 * MAX_REL)
    assert np.all(idx[lo:hi] == np.arange(1, hi - lo + 1))
    rb = rel_bias.astype(F32)
    far = rb[:, 2 * MAX_REL:]
    return jnp.concatenate([jnp.repeat(far, lo, axis=1), rb[:, 1:hi - lo + 1],
                            jnp.repeat(far, ATT_RING - hi, axis=1)], axis=1)


def _front_kernel(x1_ref, nm_ref, win_ref, wvt_ref, vones_ref, cw_ref, cb_ref, dtb_ref, a_ref, dsk_ref,
                  nw_ref, ring_ref, y_ref, o_ref, buf_ref, h_ref, st_ref, bias_ref, kring_ref, vtring_ref,
                  xs_ref, bc_ref, z_ref, q_ref, dtc_ref, acsc_ref, acst_ref):
    lt = pl.program_id(1)

    @pl.when((pl.program_id(0) == 0) & (lt == 0))
    def _():
        kring_ref[...] = jnp.zeros_like(kring_ref)
        vtring_ref[...] = jnp.zeros_like(vtring_ref)

    slot = lambda back: (lt + (KV_BLOCKS - back)) % KV_BLOCKS
    _proj_body(lt == 0, x1_ref.at[0], nm_ref, win_ref, wvt_ref, vones_ref, cw_ref, cb_ref, dtb_ref, a_ref,
               xs_ref.at[0], bc_ref.at[0], z_ref.at[0], q_ref.at[0], kring_ref.at[slot(0)],
               vtring_ref.at[slot(0)], dtc_ref.at[0], acsc_ref.at[0], acst_ref, buf_ref)
    keys = [kring_ref.at[pl.ds(slot(back), 1)] for back in (2, 1, 0)]
    vts = [vtring_ref.at[slot(back)] for back in (2, 1, 0)]
    _attn_body(q_ref, *keys, *vts, ring_ref, o_ref, st_ref, bias_ref)
    _ssd_body(xs_ref, bc_ref, dtc_ref, acsc_ref, acst_ref, z_ref, dsk_ref, nw_ref, y_ref, h_ref)


def _front(x1, nm, win, wvt, vones, cw, cb, dtb, a, dsk, nw, ring):
    assert T_SSD == TQ_ATT
    bsz, seqlen, _ = x1.shape
    t = T_SSD
    tok = lambda n: pl.BlockSpec((1, t, n), lambda b, i: (b, i, 0))
    bc_dim = CONV_DIM - SSD_INNER
    return pl.pallas_call(
        _front_kernel,
        grid=(bsz, seqlen // t),
        in_specs=[tok(D_MODEL), _const_spec((1, D_MODEL)), _const_spec((D_MODEL, PROJ_PAD)),
                  _const_spec((VT_ROWS, D_MODEL)), _const_spec((VT_ROWS, 1)),
                  _const_spec((CONV_WIDTH, CONV_DIM)), _const_spec((1, CONV_DIM)),
                  _const_spec((DT_PAD, 1)), _const_spec((DT_PAD, 1)), _const_spec((1, SSD_INNER)),
                  _const_spec((1, SSD_INNER)), _const_spec((ATT_HEADS, ATT_RING))],
        out_specs=[tok(SSD_INNER), tok(ATT_INNER)],
        out_shape=[jax.ShapeDtypeStruct((bsz, seqlen, SSD_INNER), BF16),
                   jax.ShapeDtypeStruct((bsz, seqlen, ATT_INNER), BF16)],
        scratch_shapes=[pltpu.VMEM((t + SUBLANES, CONV_DIM), F32),
                        pltpu.VMEM((SSD_GROUPS, SSD_STATE, HEADS_PER_GROUP * SSD_HEAD_DIM), F32),
                        pltpu.VMEM((ATT_HEADS // 2, KV_BLOCKS, t, 2 * t), F32),
                        pltpu.VMEM((KV_BLOCKS, ATT_HEADS, KV_BLOCKS * t, t), F32),
                        pltpu.VMEM((KV_BLOCKS, t, ATT_INNER), BF16),
                        pltpu.VMEM((KV_BLOCKS, VT_ROWS, t), BF16),
                        pltpu.VMEM((1, t, SSD_INNER), F32), pltpu.VMEM((1, t, bc_dim), BF16),
                        pltpu.VMEM((1, t, SSD_INNER), F32), pltpu.VMEM((1, t, ATT_INNER), BF16),
                        pltpu.VMEM((1, t, DT_PAD), F32), pltpu.VMEM((1, t, DT_PAD), F32),
                        pltpu.VMEM((DT_PAD, t), F32)],
        compiler_params=pltpu.CompilerParams(dimension_semantics=("arbitrary", "arbitrary"),
                                             vmem_limit_bytes=VMEM_LIMIT),
        name="front",
    )(x1, nm, win, wvt, vones, cw, cb, dtb, a, dsk, nw, ring)


def _out_ffn2_kernel(x1_ref, y_ref, att_ref, wo_ref, n2_ref, wg_ref, wu_ref, wd_ref, nf_ref,
                     out_ref, act_ref):
    mix = jnp.dot(y_ref[...], wo_ref[0:SSD_INNER, :], preferred_element_type=F32)
    mix = mix + jnp.dot(att_ref[...], wo_ref[SSD_INNER:MIX_WIDTH, :], preferred_element_type=F32)
    x2 = x1_ref[...] + mix
    h = _rms(x2, n2_ref[...]).astype(BF16)
    x3 = x2 + 0.5 * _swiglu(h, wg_ref, wu_ref, wd_ref, act_ref)
    out_ref[...] = _rms(x3, nf_ref[...])


def _out_ffn2(x1, y, att, wo, n2, wg, wu, wd, nf):
    m = x1.shape[0]
    tm = TM_FFN2
    row = lambda n: pl.BlockSpec((tm, n), lambda i: (i, 0))
    return pl.pallas_call(
        _out_ffn2_kernel,
        grid=(m // tm,),
        in_specs=[row(D_MODEL), row(SSD_INNER), row(ATT_INNER), _const_spec((MIX_WIDTH, D_MODEL)),
                  _const_spec((1, D_MODEL)), _const_spec((D_MODEL, D_FF)), _const_spec((D_MODEL, D_FF)),
                  _const_spec((D_FF, D_MODEL)), _const_spec((1, D_MODEL))],
        out_specs=row(D_MODEL),
        out_shape=jax.ShapeDtypeStruct((m, D_MODEL), F32),
        scratch_shapes=[pltpu.VMEM((tm, D_FF), BF16)],
        compiler_params=pltpu.CompilerParams(dimension_semantics=("parallel",),
                                             vmem_limit_bytes=VMEM_LIMIT),
        name="out_ffn2",
    )(x1, y, att, wo, n2, wg, wu, wd, nf)


def _layer(x, ffn1_norm, ffn1_w_gate, ffn1_w_up, ffn1_w_down, mix_norm, w_in, conv_w, conv_b, dt_bias,
           a_log, d_skip, ssd_norm, rel_bias, w_out, ffn2_norm, ffn2_w_gate, ffn2_w_up, ffn2_w_down,
           out_norm):
    bsz, seqlen, _ = x.shape
    m = bsz * seqlen
    o_xbc = SSD_INNER
    o_dt = o_xbc + CONV_DIM
    o_q = o_dt + SSD_HEADS
    o_k = o_q + ATT_INNER
    o_v = o_k + ATT_INNER
    row = lambda p: p.reshape(1, -1).astype(F32)
    pad_col = lambda p: jnp.pad(p.astype(F32), (0, DT_PAD - SSD_HEADS)).reshape(DT_PAD, 1)

    x1, (wt_bf, wo_bf, wg2_bf, wu2_bf, wd2_bf) = _ffn1(
        x.reshape(m, D_MODEL), row(ffn1_norm), ffn1_w_gate.astype(BF16), ffn1_w_up.astype(BF16),
        ffn1_w_down.astype(BF16), [w_in.T, w_out, ffn2_w_gate, ffn2_w_up, ffn2_w_down])

    win = jnp.concatenate(
        [wt_bf[o_xbc:o_dt], wt_bf[:o_xbc], wt_bf[o_q:o_k] * ATT_HEAD_DIM ** -0.5, wt_bf[o_k:o_v],
         wt_bf[o_dt:o_q], jnp.zeros((DT_PAD - SSD_HEADS, D_MODEL), BF16)], axis=0).T
    pad_rows = ((0, 0), (0, VT_HEAD_ROWS - ATT_HEAD_DIM), (0, 0))
    wvt = jnp.pad(wt_bf[o_v:].reshape(ATT_HEADS, ATT_HEAD_DIM, D_MODEL), pad_rows)
    wvt = wvt.reshape(VT_ROWS, D_MODEL)
    vones = jnp.zeros((ATT_HEADS, VT_HEAD_ROWS), F32).at[:, ATT_HEAD_DIM].set(1.0).reshape(VT_ROWS, 1)

    a_log2 = -jnp.exp(a_log.astype(F32)) * math.log2(math.e)
    y, att = _front(x1.reshape(bsz, seqlen, D_MODEL), row(mix_norm), win, wvt, vones, conv_w.astype(F32),
                    row(conv_b), pad_col(dt_bias), pad_col(a_log2), row(jnp.repeat(d_skip, SSD_HEAD_DIM)),
                    row(ssd_norm), _attn_bias_ring(rel_bias))

    out = _out_ffn2(x1, y.reshape(m, SSD_INNER), att.reshape(m, ATT_INNER), wo_bf, row(ffn2_norm), wg2_bf,
                    wu2_bf, wd2_bf, row(out_norm))
    return out.reshape(bsz, seqlen, D_MODEL)


def kernel(x, ffn1_norm, ffn1_w_gate, ffn1_w_up, ffn1_w_down, mix_norm, w_in, conv_w, conv_b, dt_bias, a_log, d_skip, ssd_norm, rel_bias, w_out, ffn2_norm, ffn2_w_gate, ffn2_w_up, ffn2_w_down, final_norm):
    depth = ffn1_norm.shape[0]
    assert depth == 1, "the fused out_ffn2 kernel applies the final norm after the only layer"
    return _layer(x, ffn1_norm[0], ffn1_w_gate[0], ffn1_w_up[0], ffn1_w_down[0], mix_norm[0], w_in[0],
                  conv_w[0], conv_b[0], dt_bias[0], a_log[0], d_skip[0], ssd_norm[0], rel_bias[0],
                  w_out[0], ffn2_norm[0], ffn2_w_gate[0], ffn2_w_up[0], ffn2_w_down[0], final_norm)
```

```python
import functools
import math

import jax
import jax.numpy as jnp
import numpy as np
from jax import lax
from jax.experimental import pallas as pl
from jax.experimental.pallas import tpu as pltpu

D_MODEL = 1024
CHUNK = 64
LEFT_CHUNKS = 8
EPS = 1e-5
D_FF = 2816
SSD_HEADS = 16
SSD_HEAD_DIM = 64
SSD_INNER = SSD_HEADS * SSD_HEAD_DIM
SSD_GROUPS = 2
HEADS_PER_GROUP = SSD_HEADS // SSD_GROUPS
SSD_STATE = 128
CONV_WIDTH = 4
CONV_DIM = SSD_INNER + 2 * SSD_GROUPS * SSD_STATE
ATT_HEADS = 8
ATT_HEAD_DIM = 64
ATT_INNER = ATT_HEADS * ATT_HEAD_DIM
MAX_REL = 256
MIX_WIDTH = SSD_INNER + ATT_INNER

LANES = 128
SUBLANES = 8
DT_PAD = LANES
PROJ_PAD = CONV_DIM + SSD_INNER + 2 * ATT_INNER + DT_PAD
BF16_SUBLANES = 16
VT_HEAD_ROWS = ATT_HEAD_DIM + BF16_SUBLANES
VT_ROWS = ATT_HEADS * VT_HEAD_ROWS

TM_FFN1 = 512
TM_FFN2 = 512
T_SSD = 256
TQ_ATT = 256
KV_BLOCKS = 3
ATT_RING = (KV_BLOCKS + 1) * TQ_ATT
DEAD_QUADRANT = {0: (0, 1), KV_BLOCKS - 1: (1, 0)}
F_CHUNKS = ((0, 1024), (1024, 2048), (2048, D_FF))
VMEM_LIMIT = 56 * 1024 * 1024

F32 = jnp.float32
BF16 = jnp.bfloat16


def _rms(x, w):
    return x * lax.rsqrt(jnp.mean(x * x, axis=-1, keepdims=True) + EPS) * w


def _silu(x):
    return x * (1.0 / (1.0 + jnp.exp(-x)))


def _swiglu(h, wg_ref, wu_ref, wd_ref, act_ref):
    for lo, hi in F_CHUNKS:
        g = jnp.dot(h, wg_ref[:, lo:hi], preferred_element_type=F32)
        u = jnp.dot(h, wu_ref[:, lo:hi], preferred_element_type=F32)
        act_ref[:, lo:hi] = (_silu(g) * u).astype(BF16)
    return jnp.dot(act_ref[...], wd_ref[...], preferred_element_type=F32)


def _ffn1_kernel(n_cast, x_ref, n1_ref, wg_ref, wu_ref, wd_ref, *refs):
    cast_in, x1_ref, cast_out, act_ref = refs[:n_cast], refs[n_cast], refs[n_cast + 1:-1], refs[-1]
    x = x_ref[...]
    h = _rms(x, n1_ref[...]).astype(BF16)
    x1_ref[...] = x + 0.5 * _swiglu(h, wg_ref, wu_ref, wd_ref, act_ref)
    for src, dst in zip(cast_in, cast_out):
        dst[...] = src[...].astype(BF16)


def _cast_block_rows(rows, steps):
    return -(-rows // (steps * BF16_SUBLANES)) * BF16_SUBLANES


def _ffn1(x, n1, wg, wu, wd, to_cast):
    m = x.shape[0]
    tm = TM_FFN1
    steps = m // tm
    row = lambda n: pl.BlockSpec((tm, n), lambda i: (i, 0))

    def cast_spec(w):
        rb = _cast_block_rows(w.shape[0], steps)
        last = -(-w.shape[0] // rb) - 1
        return pl.BlockSpec((rb, w.shape[1]), lambda i: (jnp.minimum(i, last), 0))

    cast_specs = [cast_spec(w) for w in to_cast]
    outs = pl.pallas_call(
        functools.partial(_ffn1_kernel, len(to_cast)),
        grid=(steps,),
        in_specs=[row(D_MODEL), _const_spec((1, D_MODEL)), _const_spec((D_MODEL, D_FF)),
                  _const_spec((D_MODEL, D_FF)), _const_spec((D_FF, D_MODEL))] + cast_specs,
        out_specs=[row(D_MODEL)] + cast_specs,
        out_shape=[jax.ShapeDtypeStruct((m, D_MODEL), F32)]
                  + [jax.ShapeDtypeStruct(w.shape, BF16) for w in to_cast],
        scratch_shapes=[pltpu.VMEM((tm, D_FF), BF16)],
        compiler_params=pltpu.CompilerParams(dimension_semantics=("arbitrary",),
                                             vmem_limit_bytes=VMEM_LIMIT),
        name="ffn1",
    )(x, n1, wg, wu, wd, *to_cast)
    return outs[0], outs[1:]


def _proj_body(first_of_seq, x1_ref, nm_ref, win_ref, wvt_ref, vones_ref, cw_ref, cb_ref, dtb_ref,
               a_ref, xs_ref, bc_ref, z_ref, q_ref, k_ref, vt_ref, dtc_ref, acsc_ref, acst_ref, buf_ref):
    tm = T_SSD

    @pl.when(first_of_seq)
    def _():
        buf_ref[:, 0:SUBLANES, :] = jnp.zeros((CONV_DIM // LANES, SUBLANES, LANES), F32)

    @pl.when(jnp.logical_not(first_of_seq))
    def _():
        buf_ref[:, 0:SUBLANES, :] = buf_ref[:, tm:tm + SUBLANES, :]

    h2 = _rms(x1_ref[...], nm_ref[...]).astype(BF16)

    o = PROJ_PAD - DT_PAD
    dtc_ref[...] = jnp.dot(h2, win_ref[:, o:o + DT_PAD], preferred_element_type=F32)
    xr = dtc_ref[...].T + dtb_ref[...]
    dt_t = jnp.maximum(xr, 0.0) + jnp.log1p(jnp.exp(-jnp.abs(xr)))
    acs_t = dt_t * a_ref[...]
    pos = lax.broadcasted_iota(jnp.int32, (DT_PAD, tm), 1)
    shift = 1
    while shift < tm:
        acs_t = acs_t + jnp.where(pos >= shift, pltpu.roll(acs_t, shift, 1), 0.0)
        shift *= 2
    acst_ref[...] = acs_t
    dtc_ref[...] = dt_t.T
    acsc_ref[...] = acs_t.T

    raw = jnp.dot(h2, win_ref[:, 0:CONV_DIM], preferred_element_type=F32)
    for c in range(CONV_DIM // LANES):
        buf_ref[c, SUBLANES:SUBLANES + tm, :] = raw[:, c * LANES:(c + 1) * LANES]
    vt = lax.dot_general(wvt_ref[...], h2, (((1,), (1,)), ((), ())), preferred_element_type=F32)
    vt_ref[...] = (vt + vones_ref[...]).astype(BF16)
    o = CONV_DIM
    for ref in (z_ref, q_ref, k_ref):
        n = ref.shape[-1]
        ref[...] = jnp.dot(h2, win_ref[:, o:o + n], preferred_element_type=F32).astype(ref.dtype)
        o += n

    for c in range(CONV_DIM // LANES):
        lanes = slice(c * LANES, (c + 1) * LANES)
        acc = cb_ref[:, lanes]
        for k in range(CONV_WIDTH):
            off = SUBLANES - (CONV_WIDTH - 1) + k
            acc = acc + buf_ref[c, off:off + tm, :] * cw_ref[k:k + 1, lanes]
        if c < SSD_INNER // LANES:
            xs_ref[:, lanes] = _silu(acc)
        else:
            bc_ref[:, c * LANES - SSD_INNER:(c + 1) * LANES - SSD_INNER] = _silu(acc).astype(BF16)


def _const_spec(shape):
    return pl.BlockSpec(shape, lambda *_: (0,) * len(shape), pipeline_mode=pl.Buffered(1))


def _ssd_body(xs_ref, bc_ref, dtc_ref, acsc_ref, acst_ref, z_ref, dsk_ref, nw_ref, y_ref, h_ref):
    t = T_SSD
    hb = t // 2

    @pl.when(pl.program_id(1) == 0)
    def _():
        h_ref[...] = jnp.zeros_like(h_ref)

    gn = SSD_GROUPS * SSD_STATE
    xs = xs_ref[0]
    bm_bf = bc_ref[0, :, 0:gn]
    cm_bf = bc_ref[0, :, gn:2 * gn]
    dt_c = dtc_ref[0]
    acs_c = acsc_ref[0]
    acs_t = acst_ref[...]

    causal = (lax.broadcasted_iota(jnp.int32, (hb, hb), 0) >= lax.broadcasted_iota(jnp.int32, (hb, hb), 1))
    first = lambda rows: lax.broadcasted_iota(jnp.int32, (rows, 2 * SSD_HEAD_DIM), 1) < SSD_HEAD_DIM
    lo, lo_st, lo_row = first(t), first(SSD_STATE), first(1)
    pw = 2 * SSD_HEAD_DIM
    ys = []
    for g in range(SSD_GROUPS):
        n0 = g * SSD_STATE
        cg = cm_bf[:, n0:n0 + SSD_STATE]
        cbm = lax.dot_general(cg, bm_bf[:, n0:n0 + SSD_STATE], (((1,), (1,)), ((), ())),
                              preferred_element_type=F32)
        cbm_bf = cbm.astype(BF16)
        bt = bm_bf[:, n0:n0 + SSD_STATE].astype(F32).T.astype(BF16)
        hprev = h_ref[g]
        yoff = jnp.dot(cg, hprev.astype(BF16), preferred_element_type=F32)
        for pr in range(HEADS_PER_GROUP // 2):
            h0 = g * HEADS_PER_GROUP + 2 * pr
            h1 = h0 + 1
            l0 = pr * pw
            x_pair = xs[:, h0 * SSD_HEAD_DIM:h0 * SSD_HEAD_DIM + pw]
            xdt = (x_pair * jnp.where(lo, dt_c[:, h0:h0 + 1], dt_c[:, h1:h1 + 1])).astype(BF16)
            lhs = []
            for hd in (h0, h1):
                col = acs_c[:, hd:hd + 1]
                row = acs_t[hd:hd + 1, :]
                d00 = jnp.exp2(jnp.where(causal, col[:hb] - row[:, :hb], -jnp.inf).astype(BF16))
                d10 = jnp.exp2((col[hb:] - row[:, :hb]).astype(BF16))
                d11 = jnp.exp2(jnp.where(causal, col[hb:] - row[:, hb:], -jnp.inf).astype(BF16))
                top = jnp.concatenate([cbm_bf[:hb, :hb] * d00, jnp.zeros((hb, hb), BF16)], axis=1)
                bot = jnp.concatenate([cbm_bf[hb:, :hb] * d10, cbm_bf[hb:, hb:] * d11], axis=1)
                lhs += [top, bot]
            lasts = [acs_t[hd:hd + 1, t - 1:t] for hd in (h0, h1)]
            for hd, last in zip((h0, h1), lasts):
                lhs.append(bt * jnp.exp2((last - acs_t[hd:hd + 1, :]).astype(BF16)))
            r = jnp.dot(jnp.concatenate(lhs, axis=0), xdt, preferred_element_type=F32)
            yd = jnp.where(lo, r[0:t], r[t:2 * t])
            st = jnp.where(lo_st, r[2 * t:2 * t + SSD_STATE], r[2 * t + SSD_STATE:])
            ecol = jnp.exp2(jnp.where(lo, acs_c[:, h0:h0 + 1], acs_c[:, h1:h1 + 1]))
            ys.append(yd + yoff[:, l0:l0 + pw] * ecol)
            elast = jnp.exp2(jnp.where(lo_row, lasts[0], lasts[1]))
            h_ref[g, :, l0:l0 + pw] = hprev[:, l0:l0 + pw] * elast + st

    y = jnp.concatenate(ys, axis=1) + xs * dsk_ref[...]
    y = y * _silu(z_ref[0])
    y_ref[0] = _rms(y, nw_ref[...]).astype(y_ref.dtype)


def _attn_body(q_ref, k0_ref, k1_ref, k2_ref, vt0_ref, vt1_ref, vt2_ref, ring_ref, o_ref, st_ref,
               bias_ref):
    tq = TQ_ATT
    nk = KV_BLOCKS * tq
    lt = pl.program_id(1)

    @pl.when((pl.program_id(0) == 0) & (lt == 0))
    def _():
        kpos = lax.broadcasted_iota(jnp.int32, (nk, tq), 0)
        qpos = lax.broadcasted_iota(jnp.int32, (nk, tq), 1) + (KV_BLOCKS - 1) * tq
        dchunk = qpos // CHUNK - kpos // CHUNK
        neg = jnp.finfo(F32).min
        for hd in range(ATT_HEADS):
            ring = jnp.broadcast_to(ring_ref[hd:hd + 1, :], (nk, ATT_RING))
            toep = pltpu.roll(ring, 0, 1, stride=1, stride_axis=0)[:, 0:tq]
            band = jnp.where(dchunk >= 0, jnp.where(dchunk <= LEFT_CHUNKS, toep, neg), neg)
            for n in range(KV_BLOCKS):
                bias_ref[n, hd] = jnp.where(kpos >= (KV_BLOCKS - 1 - n) * tq, band, neg)

    case = jnp.minimum(lt, KV_BLOCKS - 1)
    pw = 2 * ATT_HEAD_DIM
    lo = lax.broadcasted_iota(jnp.int32, (tq, pw), 1) < ATT_HEAD_DIM
    nt = (((1,), (1,)), ((), ()))
    hk, hq = tq // 2, tq // 2
    for hp in range(ATT_HEADS // 2):
        c0 = hp * pw
        qp = q_ref[0, :, c0:c0 + pw]
        zero = jnp.zeros_like(qp)
        qs = jnp.concatenate([jnp.where(lo, qp, zero), jnp.where(lo, zero, qp)], axis=0)
        for j, kr in enumerate((k0_ref, k1_ref, k2_ref)):
            st_ref[hp, j] = lax.dot_general(kr[0, :, c0:c0 + pw], qs, nt, preferred_element_type=F32)
    for hp in range(ATT_HEADS // 2):
        c0 = hp * pw
        outs = []
        for e in range(2):
            hd = 2 * hp + e
            m = None
            ot = None
            for j, vr in reversed(list(enumerate((vt0_ref, vt1_ref, vt2_ref)))):
                def scores(r0, r1, q0, q1):
                    return (st_ref[hp, j, r0:r1, e * tq + q0:e * tq + q1]
                            + bias_ref[case, hd, j * tq + r0:j * tq + r1, q0:q1])
                if j not in DEAD_QUADRANT:
                    s = scores(0, tq, 0, tq)
                    mj = jnp.max(s, axis=0, keepdims=True)
                    m_new = mj if m is None else jnp.maximum(m, mj)
                    p = jnp.exp((s - m_new).astype(BF16))
                else:
                    rh, qh = DEAD_QUADRANT[j]
                    dead_q = (KV_BLOCKS - 1) * tq + qh * hq + np.arange(hq)
                    dead_k = j * tq + rh * hk + np.arange(hk)
                    dist = dead_q[None, :] // CHUNK - dead_k[:, None] // CHUNK
                    assert not np.any((dist >= 0) & (dist <= LEFT_CHUNKS)), "quadrant is inside the band"
                    full = scores((1 - rh) * hk, (2 - rh) * hk, 0, tq)
                    part = scores(rh * hk, (rh + 1) * hk, (1 - qh) * hq, (2 - qh) * hq)
                    mf = jnp.max(full, axis=0, keepdims=True)
                    mp = jnp.max(part, axis=0, keepdims=True)
                    neg = jnp.full((1, hq), jnp.finfo(F32).min, F32)
                    mp = jnp.concatenate([neg, mp] if qh == 0 else [mp, neg], axis=1)
                    mj = jnp.maximum(mf, mp)
                    m_new = mj if m is None else jnp.maximum(m, mj)
                    pf = jnp.exp((full - m_new).astype(BF16))
                    pp = jnp.exp((part - m_new[:, (1 - qh) * hq:(2 - qh) * hq]).astype(BF16))
                    zeros = jnp.zeros((hk, hq), BF16)
                    pp = jnp.concatenate([zeros, pp] if qh == 0 else [pp, zeros], axis=1)
                    p = jnp.concatenate([pp, pf] if rh == 0 else [pf, pp], axis=0)
                oj = jnp.dot(vr[hd * VT_HEAD_ROWS:(hd + 1) * VT_HEAD_ROWS, :], p,
                             preferred_element_type=F32)
                ot = oj if ot is None else ot * jnp.exp(m - m_new) + oj
                m = m_new
            outs.append(ot[0:ATT_HEAD_DIM] / ot[ATT_HEAD_DIM:ATT_HEAD_DIM + 1])
        o_ref[0, :, c0:c0 + pw] = jnp.concatenate(outs, axis=0).T.astype(o_ref.dtype)


def _attn_bias_ring(rel_bias):
    q0 = (KV_BLOCKS - 1) * TQ_ATT
    j = np.arange(ATT_RING)
    e = np.where(j <= TQ_ATT, -j, ATT_RING - j)
    idx = np.clip(q0 - e, -MAX_REL, MAX_REL) + MAX_REL
    lo, hi = q0 - MAX_REL + 1, ATT_RING - (q0 - MAX_REL)
    assert np.all(idx[:lo] == 2 * MAX_REL) and np.all(idx[hi:] == 2 * MAX_REL)
    assert np.all(idx[lo:hi] == np.arange(1, hi - lo + 1))
    rb = rel_bias.astype(F32)
    far = rb[:, 2 * MAX_REL:]
    return jnp.concatenate([jnp.repeat(far, lo, axis=1), rb[:, 1:hi - lo + 1],
                            jnp.repeat(far, ATT_RING - hi, axis=1)], axis=1)


def _front_kernel(x1_ref, nm_ref, win_ref, wvt_ref, vones_ref, cw_ref, cb_ref, dtb_ref, a_ref, dsk_ref,
                  nw_ref, ring_ref, y_ref, o_ref, buf_ref, h_ref, st_ref, bias_ref, kring_ref, vtring_ref,
                  xs_ref, bc_ref, z_ref, q_ref, dtc_ref, acsc_ref, acst_ref):
    lt = pl.program_id(1)

    @pl.when((pl.program_id(0) == 0) & (lt == 0))
    def _():
        kring_ref[...] = jnp.zeros_like(kring_ref)
        vtring_ref[...] = jnp.zeros_like(vtring_ref)

    slot = lambda back: (lt + (KV_BLOCKS - back)) % KV_BLOCKS
    _proj_body(lt == 0, x1_ref.at[0], nm_ref, win_ref, wvt_ref, vones_ref, cw_ref, cb_ref, dtb_ref, a_ref,
               xs_ref.at[0], bc_ref.at[0], z_ref.at[0], q_ref.at[0], kring_ref.at[slot(0)],
               vtring_ref.at[slot(0)], dtc_ref.at[0], acsc_ref.at[0], acst_ref, buf_ref)
    keys = [kring_ref.at[pl.ds(slot(back), 1)] for back in (2, 1, 0)]
    vts = [vtring_ref.at[slot(back)] for back in (2, 1, 0)]
    _attn_body(q_ref, *keys, *vts, ring_ref, o_ref, st_ref, bias_ref)
    _ssd_body(xs_ref, bc_ref, dtc_ref, acsc_ref, acst_ref, z_ref, dsk_ref, nw_ref, y_ref, h_ref)


def _front(x1, nm, win, wvt, vones, cw, cb, dtb, a, dsk, nw, ring):
    assert T_SSD == TQ_ATT
    bsz, seqlen, _ = x1.shape
    t = T_SSD
    tok = lambda n: pl.BlockSpec((1, t, n), lambda b, i: (b, i, 0))
    bc_dim = CONV_DIM - SSD_INNER
    return pl.pallas_call(
        _front_kernel,
        grid=(bsz, seqlen // t),
        in_specs=[tok(D_MODEL), _const_spec((1, D_MODEL)), _const_spec((D_MODEL, PROJ_PAD)),
                  _const_spec((VT_ROWS, D_MODEL)), _const_spec((VT_ROWS, 1)),
                  _const_spec((CONV_WIDTH, CONV_DIM)), _const_spec((1, CONV_DIM)),
                  _const_spec((DT_PAD, 1)), _const_spec((DT_PAD, 1)), _const_spec((1, SSD_INNER)),
                  _const_spec((1, SSD_INNER)), _const_spec((ATT_HEADS, ATT_RING))],
        out_specs=[tok(SSD_INNER), tok(ATT_INNER)],
        out_shape=[jax.ShapeDtypeStruct((bsz, seqlen, SSD_INNER), BF16),
                   jax.ShapeDtypeStruct((bsz, seqlen, ATT_INNER), BF16)],
        scratch_shapes=[pltpu.VMEM((CONV_DIM // LANES, t + SUBLANES, LANES), F32),
                        pltpu.VMEM((SSD_GROUPS, SSD_STATE, HEADS_PER_GROUP * SSD_HEAD_DIM), F32),
                        pltpu.VMEM((ATT_HEADS // 2, KV_BLOCKS, t, 2 * t), F32),
                        pltpu.VMEM((KV_BLOCKS, ATT_HEADS, KV_BLOCKS * t, t), F32),
                        pltpu.VMEM((KV_BLOCKS, t, ATT_INNER), BF16),
                        pltpu.VMEM((KV_BLOCKS, VT_ROWS, t), BF16),
                        pltpu.VMEM((1, t, SSD_INNER), F32), pltpu.VMEM((1, t, bc_dim), BF16),
                        pltpu.VMEM((1, t, SSD_INNER), F32), pltpu.VMEM((1, t, ATT_INNER), BF16),
                        pltpu.VMEM((1, t, DT_PAD), F32), pltpu.VMEM((1, t, DT_PAD), F32),
                        pltpu.VMEM((DT_PAD, t), F32)],
        compiler_params=pltpu.CompilerParams(dimension_semantics=("arbitrary", "arbitrary"),
                                             vmem_limit_bytes=VMEM_LIMIT),
        name="front",
    )(x1, nm, win, wvt, vones, cw, cb, dtb, a, dsk, nw, ring)


def _out_ffn2_kernel(x1_ref, y_ref, att_ref, wo_ref, n2_ref, wg_ref, wu_ref, wd_ref, nf_ref,
                     out_ref, act_ref):
    mix = jnp.dot(y_ref[...], wo_ref[0:SSD_INNER, :], preferred_element_type=F32)
    mix = mix + jnp.dot(att_ref[...], wo_ref[SSD_INNER:MIX_WIDTH, :], preferred_element_type=F32)
    x2 = x1_ref[...] + mix
    h = _rms(x2, n2_ref[...]).astype(BF16)
    x3 = x2 + 0.5 * _swiglu(h, wg_ref, wu_ref, wd_ref, act_ref)
    out_ref[...] = _rms(x3, nf_ref[...])


def _out_ffn2(x1, y, att, wo, n2, wg, wu, wd, nf):
    m = x1.shape[0]
    tm = TM_FFN2
    row = lambda n: pl.BlockSpec((tm, n), lambda i: (i, 0))
    return pl.pallas_call(
        _out_ffn2_kernel,
        grid=(m // tm,),
        in_specs=[row(D_MODEL), row(SSD_INNER), row(ATT_INNER), _const_spec((MIX_WIDTH, D_MODEL)),
                  _const_spec((1, D_MODEL)), _const_spec((D_MODEL, D_FF)), _const_spec((D_MODEL, D_FF)),
                  _const_spec((D_FF, D_MODEL)), _const_spec((1, D_MODEL))],
        out_specs=row(D_MODEL),
        out_shape=jax.ShapeDtypeStruct((m, D_MODEL), F32),
        scratch_shapes=[pltpu.VMEM((tm, D_FF), BF16)],
        compiler_params=pltpu.CompilerParams(dimension_semantics=("parallel",),
                                             vmem_limit_bytes=VMEM_LIMIT),
        name="out_ffn2",
    )(x1, y, att, wo, n2, wg, wu, wd, nf)


def _layer(x, ffn1_norm, ffn1_w_gate, ffn1_w_up, ffn1_w_down, mix_norm, w_in, conv_w, conv_b, dt_bias,
           a_log, d_skip, ssd_norm, rel_bias, w_out, ffn2_norm, ffn2_w_gate, ffn2_w_up, ffn2_w_down,
           out_norm):
    bsz, seqlen, _ = x.shape
    m = bsz * seqlen
    o_xbc = SSD_INNER
    o_dt = o_xbc + CONV_DIM
    o_q = o_dt + SSD_HEADS
    o_k = o_q + ATT_INNER
    o_v = o_k + ATT_INNER
    row = lambda p: p.reshape(1, -1).astype(F32)
    pad_col = lambda p: jnp.pad(p.astype(F32), (0, DT_PAD - SSD_HEADS)).reshape(DT_PAD, 1)

    x1, (wt_bf, wo_bf, wg2_bf, wu2_bf, wd2_bf) = _ffn1(
        x.reshape(m, D_MODEL), row(ffn1_norm), ffn1_w_gate.astype(BF16), ffn1_w_up.astype(BF16),
        ffn1_w_down.astype(BF16), [w_in.T, w_out, ffn2_w_gate, ffn2_w_up, ffn2_w_down])

    win = jnp.concatenate(
        [wt_bf[o_xbc:o_dt], wt_bf[:o_xbc], wt_bf[o_q:o_k] * ATT_HEAD_DIM ** -0.5, wt_bf[o_k:o_v],
         wt_bf[o_dt:o_q], jnp.zeros((DT_PAD - SSD_HEADS, D_MODEL), BF16)], axis=0).T
    pad_rows = ((0, 0), (0, VT_HEAD_ROWS - ATT_HEAD_DIM), (0, 0))
    wvt = jnp.pad(wt_bf[o_v:].reshape(ATT_HEADS, ATT_HEAD_DIM, D_MODEL), pad_rows)
    wvt = wvt.reshape(VT_ROWS, D_MODEL)
    vones = jnp.zeros((ATT_HEADS, VT_HEAD_ROWS), F32).at[:, ATT_HEAD_DIM].set(1.0).reshape(VT_ROWS, 1)

    a_log2 = -jnp.exp(a_log.astype(F32)) * math.log2(math.e)
    y, att = _front(x1.reshape(bsz, seqlen, D_MODEL), row(mix_norm), win, wvt, vones, conv_w.astype(F32),
                    row(conv_b), pad_col(dt_bias), pad_col(a_log2), row(jnp.repeat(d_skip, SSD_HEAD_DIM)),
                    row(ssd_norm), _attn_bias_ring(rel_bias))

    out = _out_ffn2(x1, y.reshape(m, SSD_INNER), att.reshape(m, ATT_INNER), wo_bf, row(ffn2_norm), wg2_bf,
                    wu2_bf, wd2_bf, row(out_norm))
    return out.reshape(bsz, seqlen, D_MODEL)


def kernel(x, ffn1_norm, ffn1_w_gate, ffn1_w_up, ffn1_w_down, mix_norm, w_in, conv_w, conv_b, dt_bias, a_log, d_skip, ssd_norm, rel_bias, w_out, ffn2_norm, ffn2_w_gate, ffn2_w_up, ffn2_w_down, final_norm):
    depth = ffn1_norm.shape[0]
    assert depth == 1, "the fused out_ffn2 kernel applies the final norm after the only layer"
    return _layer(x, ffn1_norm[0], ffn1_w_gate[0], ffn1_w_up[0], ffn1_w_down[0], mix_norm[0], w_in[0],
                  conv_w[0], conv_b[0], dt_bias[0], a_log[0], d_skip[0], ssd_norm[0], rel_bias[0],
                  w_out[0], ffn2_norm[0], ffn2_w_gate[0], ffn2_w_up[0], ffn2_w_down[0], final_norm)
```

```python
import functools
import math

import jax
import jax.numpy as jnp
import numpy as np
from jax import lax
from jax.experimental import pallas as pl
from jax.experimental.pallas import tpu as pltpu

D_MODEL = 1024
CHUNK = 64
LEFT_CHUNKS = 8
EPS = 1e-5
D_FF = 2816
SSD_HEADS = 16
SSD_HEAD_DIM = 64
SSD_INNER = SSD_HEADS * SSD_HEAD_DIM
SSD_GROUPS = 2
HEADS_PER_GROUP = SSD_HEADS // SSD_GROUPS
SSD_STATE = 128
CONV_WIDTH = 4
CONV_DIM = SSD_INNER + 2 * SSD_GROUPS * SSD_STATE
ATT_HEADS = 8
ATT_HEAD_DIM = 64
ATT_INNER = ATT_HEADS * ATT_HEAD_DIM
MAX_REL = 256
MIX_WIDTH = SSD_INNER + ATT_INNER

LANES = 128
SUBLANES = 8
DT_PAD = LANES
PROJ_PAD = CONV_DIM + SSD_INNER + 2 * ATT_INNER + DT_PAD
BF16_SUBLANES = 16
VT_HEAD_ROWS = ATT_HEAD_DIM + BF16_SUBLANES
VT_ROWS = ATT_HEADS * VT_HEAD_ROWS

TM_FFN1 = 1024
TM_FFN2 = 512
T_SSD = 256
TQ_ATT = 256
KV_BLOCKS = 3
ATT_RING = (KV_BLOCKS + 1) * TQ_ATT
DEAD_QUADRANT = {0: (0, 1), KV_BLOCKS - 1: (1, 0)}
F_CHUNKS = ((0, 1024), (1024, 2048), (2048, D_FF))
VMEM_LIMIT = 56 * 1024 * 1024

F32 = jnp.float32
BF16 = jnp.bfloat16


def _rms(x, w):
    return x * lax.rsqrt(jnp.mean(x * x, axis=-1, keepdims=True) + EPS) * w


def _silu(x):
    return x * (1.0 / (1.0 + jnp.exp(-x)))


def _swiglu(h, wg_ref, wu_ref, wd_ref, act_ref):
    for lo, hi in F_CHUNKS:
        g = jnp.dot(h, wg_ref[:, lo:hi], preferred_element_type=F32)
        u = jnp.dot(h, wu_ref[:, lo:hi], preferred_element_type=F32)
        act_ref[:, lo:hi] = (_silu(g) * u).astype(BF16)
    return jnp.dot(act_ref[...], wd_ref[...], preferred_element_type=F32)


def _ffn1_kernel(n_cast, x_ref, n1_ref, wg_ref, wu_ref, wd_ref, *refs):
    cast_in, x1_ref, cast_out, act_ref = refs[:n_cast], refs[n_cast], refs[n_cast + 1:-1], refs[-1]
    x = x_ref[...]
    h = _rms(x, n1_ref[...]).astype(BF16)
    x1_ref[...] = x + 0.5 * _swiglu(h, wg_ref, wu_ref, wd_ref, act_ref)
    for src, dst in zip(cast_in, cast_out):
        dst[...] = src[...].astype(BF16)


def _cast_block_rows(rows, steps):
    return -(-rows // (steps * BF16_SUBLANES)) * BF16_SUBLANES


def _ffn1(x, n1, wg, wu, wd, to_cast):
    m = x.shape[0]
    tm = TM_FFN1
    steps = m // tm
    row = lambda n: pl.BlockSpec((tm, n), lambda i: (i, 0))

    def cast_spec(w):
        rb = _cast_block_rows(w.shape[0], steps)
        last = -(-w.shape[0] // rb) - 1
        return pl.BlockSpec((rb, w.shape[1]), lambda i: (jnp.minimum(i, last), 0))

    cast_specs = [cast_spec(w) for w in to_cast]
    outs = pl.pallas_call(
        functools.partial(_ffn1_kernel, len(to_cast)),
        grid=(steps,),
        in_specs=[row(D_MODEL), _const_spec((1, D_MODEL)), _const_spec((D_MODEL, D_FF)),
                  _const_spec((D_MODEL, D_FF)), _const_spec((D_FF, D_MODEL))] + cast_specs,
        out_specs=[row(D_MODEL)] + cast_specs,
        out_shape=[jax.ShapeDtypeStruct((m, D_MODEL), F32)]
                  + [jax.ShapeDtypeStruct(w.shape, BF16) for w in to_cast],
        scratch_shapes=[pltpu.VMEM((tm, D_FF), BF16)],
        compiler_params=pltpu.CompilerParams(dimension_semantics=("arbitrary",),
                                             vmem_limit_bytes=VMEM_LIMIT),
        name="ffn1",
    )(x, n1, wg, wu, wd, *to_cast)
    return outs[0], outs[1:]


def _proj_body(first_of_seq, x1_ref, nm_ref, win_ref, wvt_ref, vones_ref, cw_ref, cb_ref, dtb_ref,
               a_ref, xs_ref, bc_ref, z_ref, q_ref, k_ref, vt_ref, dtc_ref, acsc_ref, acst_ref, buf_ref):
    tm = T_SSD

    @pl.when(first_of_seq)
    def _():
        buf_ref[:, 0:SUBLANES, :] = jnp.zeros((CONV_DIM // LANES, SUBLANES, LANES), F32)

    @pl.when(jnp.logical_not(first_of_seq))
    def _():
        buf_ref[:, 0:SUBLANES, :] = buf_ref[:, tm:tm + SUBLANES, :]

    h2 = _rms(x1_ref[...], nm_ref[...]).astype(BF16)

    o = PROJ_PAD - DT_PAD
    dtc_ref[...] = jnp.dot(h2, win_ref[:, o:o + DT_PAD], preferred_element_type=F32)
    xr = dtc_ref[...].T + dtb_ref[...]
    dt_t = jnp.maximum(xr, 0.0) + jnp.log1p(jnp.exp(-jnp.abs(xr)))
    acs_t = dt_t * a_ref[...]
    pos = lax.broadcasted_iota(jnp.int32, (DT_PAD, tm), 1)
    shift = 1
    while shift < tm:
        acs_t = acs_t + jnp.where(pos >= shift, pltpu.roll(acs_t, shift, 1), 0.0)
        shift *= 2
    acst_ref[...] = acs_t
    dtc_ref[...] = dt_t.T
    acsc_ref[...] = acs_t.T

    raw = jnp.dot(h2, win_ref[:, 0:CONV_DIM], preferred_element_type=F32)
    for c in range(CONV_DIM // LANES):
        buf_ref[c, SUBLANES:SUBLANES + tm, :] = raw[:, c * LANES:(c + 1) * LANES]
    vt = lax.dot_general(wvt_ref[...], h2, (((1,), (1,)), ((), ())), preferred_element_type=F32)
    vt_ref[...] = (vt + vones_ref[...]).astype(BF16)
    o = CONV_DIM
    for ref in (z_ref, q_ref, k_ref):
        n = ref.shape[-1]
        ref[...] = jnp.dot(h2, win_ref[:, o:o + n], preferred_element_type=F32).astype(ref.dtype)
        o += n

    for c in range(CONV_DIM // LANES):
        lanes = slice(c * LANES, (c + 1) * LANES)
        acc = cb_ref[:, lanes]
        for k in range(CONV_WIDTH):
            off = SUBLANES - (CONV_WIDTH - 1) + k
            acc = acc + buf_ref[c, off:off + tm, :] * cw_ref[k:k + 1, lanes]
        if c < SSD_INNER // LANES:
            xs_ref[:, lanes] = _silu(acc)
        else:
            bc_ref[:, c * LANES - SSD_INNER:(c + 1) * LANES - SSD_INNER] = _silu(acc).astype(BF16)


def _const_spec(shape):
    return pl.BlockSpec(shape, lambda *_: (0,) * len(shape), pipeline_mode=pl.Buffered(1))


def _ssd_body(xs_ref, bc_ref, dtc_ref, acsc_ref, acst_ref, z_ref, dsk_ref, nw_ref, y_ref, h_ref):
    t = T_SSD
    hb = t // 2

    @pl.when(pl.program_id(1) == 0)
    def _():
        h_ref[...] = jnp.zeros_like(h_ref)

    gn = SSD_GROUPS * SSD_STATE
    xs = xs_ref[0]
    bm_bf = bc_ref[0, :, 0:gn]
    cm_bf = bc_ref[0, :, gn:2 * gn]
    dt_c = dtc_ref[0]
    acs_c = acsc_ref[0]
    acs_t = acst_ref[...]

    causal = (lax.broadcasted_iota(jnp.int32, (hb, hb), 0) >= lax.broadcasted_iota(jnp.int32, (hb, hb), 1))
    first = lambda rows: lax.broadcasted_iota(jnp.int32, (rows, 2 * SSD_HEAD_DIM), 1) < SSD_HEAD_DIM
    lo, lo_st, lo_row = first(t), first(SSD_STATE), first(1)
    pw = 2 * SSD_HEAD_DIM
    ys = []
    for g in range(SSD_GROUPS):
        n0 = g * SSD_STATE
        cg = cm_bf[:, n0:n0 + SSD_STATE]
        cbm = lax.dot_general(cg, bm_bf[:, n0:n0 + SSD_STATE], (((1,), (1,)), ((), ())),
                              preferred_element_type=F32)
        cbm_bf = cbm.astype(BF16)
        bt = bm_bf[:, n0:n0 + SSD_STATE].astype(F32).T.astype(BF16)
        hprev = h_ref[g]
        yoff = jnp.dot(cg, hprev.astype(BF16), preferred_element_type=F32)
        for pr in range(HEADS_PER_GROUP // 2):
            h0 = g * HEADS_PER_GROUP + 2 * pr
            h1 = h0 + 1
            l0 = pr * pw
            x_pair = xs[:, h0 * SSD_HEAD_DIM:h0 * SSD_HEAD_DIM + pw]
            xdt = (x_pair * jnp.where(lo, dt_c[:, h0:h0 + 1], dt_c[:, h1:h1 + 1])).astype(BF16)
            lhs = []
            for hd in (h0, h1):
                col = acs_c[:, hd:hd + 1]
                row = acs_t[hd:hd + 1, :]
                d00 = jnp.exp2(jnp.where(causal, col[:hb] - row[:, :hb], -jnp.inf).astype(BF16))
                d10 = jnp.exp2((col[hb:] - row[:, :hb]).astype(BF16))
                d11 = jnp.exp2(jnp.where(causal, col[hb:] - row[:, hb:], -jnp.inf).astype(BF16))
                top = jnp.concatenate([cbm_bf[:hb, :hb] * d00, jnp.zeros((hb, hb), BF16)], axis=1)
                bot = jnp.concatenate([cbm_bf[hb:, :hb] * d10, cbm_bf[hb:, hb:] * d11], axis=1)
                lhs += [top, bot]
            lasts = [acs_t[hd:hd + 1, t - 1:t] for hd in (h0, h1)]
            for hd, last in zip((h0, h1), lasts):
                lhs.append(bt * jnp.exp2((last - acs_t[hd:hd + 1, :]).astype(BF16)))
            r = jnp.dot(jnp.concatenate(lhs, axis=0), xdt, preferred_element_type=F32)
            yd = jnp.where(lo, r[0:t], r[t:2 * t])
            st = jnp.where(lo_st, r[2 * t:2 * t + SSD_STATE], r[2 * t + SSD_STATE:])
            ecol = jnp.exp2(jnp.where(lo, acs_c[:, h0:h0 + 1], acs_c[:, h1:h1 + 1]))
            ys.append(yd + yoff[:, l0:l0 + pw] * ecol)
            elast = jnp.exp2(jnp.where(lo_row, lasts[0], lasts[1]))
            h_ref[g, :, l0:l0 + pw] = hprev[:, l0:l0 + pw] * elast + st

    y = jnp.concatenate(ys, axis=1) + xs * dsk_ref[...]
    y = y * _silu(z_ref[0])
    y_ref[0] = _rms(y, nw_ref[...]).astype(y_ref.dtype)


def _attn_body(q_ref, k0_ref, k1_ref, k2_ref, vt0_ref, vt1_ref, vt2_ref, ring_ref, o_ref, st_ref,
               bias_ref):
    tq = TQ_ATT
    nk = KV_BLOCKS * tq
    lt = pl.program_id(1)

    @pl.when((pl.program_id(0) == 0) & (lt == 0))
    def _():
        kpos = lax.broadcasted_iota(jnp.int32, (nk, tq), 0)
        qpos = lax.broadcasted_iota(jnp.int32, (nk, tq), 1) + (KV_BLOCKS - 1) * tq
        dchunk = qpos // CHUNK - kpos // CHUNK
        neg = jnp.finfo(F32).min
        for hd in range(ATT_HEADS):
            ring = jnp.broadcast_to(ring_ref[hd:hd + 1, :], (nk, ATT_RING))
            toep = pltpu.roll(ring, 0, 1, stride=1, stride_axis=0)[:, 0:tq]
            band = jnp.where(dchunk >= 0, jnp.where(dchunk <= LEFT_CHUNKS, toep, neg), neg)
            for n in range(KV_BLOCKS):
                bias_ref[n, hd] = jnp.where(kpos >= (KV_BLOCKS - 1 - n) * tq, band, neg)

    case = jnp.minimum(lt, KV_BLOCKS - 1)
    pw = 2 * ATT_HEAD_DIM
    lo = lax.broadcasted_iota(jnp.int32, (tq, pw), 1) < ATT_HEAD_DIM
    nt = (((1,), (1,)), ((), ()))
    hk, hq = tq // 2, tq // 2
    for hp in range(ATT_HEADS // 2):
        c0 = hp * pw
        qp = q_ref[0, :, c0:c0 + pw]
        zero = jnp.zeros_like(qp)
        qs = jnp.concatenate([jnp.where(lo, qp, zero), jnp.where(lo, zero, qp)], axis=0)
        for j, kr in enumerate((k0_ref, k1_ref, k2_ref)):
            st_ref[hp, j] = lax.dot_general(kr[0, :, c0:c0 + pw], qs, nt, preferred_element_type=F32)
    for hp in range(ATT_HEADS // 2):
        c0 = hp * pw
        outs = []
        for e in range(2):
            hd = 2 * hp + e
            m = None
            ot = None
            for j, vr in reversed(list(enumerate((vt0_ref, vt1_ref, vt2_ref)))):
                def scores(r0, r1, q0, q1):
                    return (st_ref[hp, j, r0:r1, e * tq + q0:e * tq + q1]
                            + bias_ref[case, hd, j * tq + r0:j * tq + r1, q0:q1])
                if j not in DEAD_QUADRANT:
                    s = scores(0, tq, 0, tq)
                    mj = jnp.max(s, axis=0, keepdims=True)
                    m_new = mj if m is None else jnp.maximum(m, mj)
                    p = jnp.exp((s - m_new).astype(BF16))
                else:
                    rh, qh = DEAD_QUADRANT[j]
                    dead_q = (KV_BLOCKS - 1) * tq + qh * hq + np.arange(hq)
                    dead_k = j * tq + rh * hk + np.arange(hk)
                    dist = dead_q[None, :] // CHUNK - dead_k[:, None] // CHUNK
                    assert not np.any((dist >= 0) & (dist <= LEFT_CHUNKS)), "quadrant is inside the band"
                    full = scores((1 - rh) * hk, (2 - rh) * hk, 0, tq)
                    part = scores(rh * hk, (rh + 1) * hk, (1 - qh) * hq, (2 - qh) * hq)
                    mf = jnp.max(full, axis=0, keepdims=True)
                    mp = jnp.max(part, axis=0, keepdims=True)
                    neg = jnp.full((1, hq), jnp.finfo(F32).min, F32)
                    mp = jnp.concatenate([neg, mp] if qh == 0 else [mp, neg], axis=1)
                    mj = jnp.maximum(mf, mp)
                    m_new = mj if m is None else jnp.maximum(m, mj)
                    pf = jnp.exp((full - m_new).astype(BF16))
                    pp = jnp.exp((part - m_new[:, (1 - qh) * hq:(2 - qh) * hq]).astype(BF16))
                    zeros = jnp.zeros((hk, hq), BF16)
                    pp = jnp.concatenate([zeros, pp] if qh == 0 else [pp, zeros], axis=1)
                    p = jnp.concatenate([pp, pf] if rh == 0 else [pf, pp], axis=0)
                oj = jnp.dot(vr[hd * VT_HEAD_ROWS:(hd + 1) * VT_HEAD_ROWS, :], p,
                             preferred_element_type=F32)
                ot = oj if ot is None else ot * jnp.exp(m - m_new) + oj
                m = m_new
            outs.append(ot[0:ATT_HEAD_DIM] / ot[ATT_HEAD_DIM:ATT_HEAD_DIM + 1])
        o_ref[0, :, c0:c0 + pw] = jnp.concatenate(outs, axis=0).T.astype(o_ref.dtype)


def _attn_bias_ring(rel_bias):
    q0 = (KV_BLOCKS - 1) * TQ_ATT
    j = np.arange(ATT_RING)
    e = np.where(j <= TQ_ATT, -j, ATT_RING - j)
    idx = np.clip(q0 - e, -MAX_REL, MAX_REL) + MAX_REL
    lo, hi = q0 - MAX_REL + 1, ATT_RING - (q0 - MAX_REL)
    assert np.all(idx[:lo] == 2 * MAX_REL) and np.all(idx[hi:] == 2 * MAX_REL)
    assert np.all(idx[lo:hi] == np.arange(1, hi - lo + 1))
    rb = rel_bias.astype(F32)
    far = rb[:, 2 * MAX_REL:]
    return jnp.concatenate([jnp.repeat(far, lo, axis=1), rb[:, 1:hi - lo + 1],
                            jnp.repeat(far, ATT_RING - hi, axis=1)], axis=1)


def _front_kernel(x1_ref, nm_ref, win_ref, wvt_ref, vones_ref, cw_ref, cb_ref, dtb_ref, a_ref, dsk_ref,
                  nw_ref, ring_ref, y_ref, o_ref, buf_ref, h_ref, st_ref, bias_ref, kring_ref, vtring_ref,
                  xs_ref, bc_ref, z_ref, q_ref, dtc_ref, acsc_ref, acst_ref):
    lt = pl.program_id(1)

    @pl.when((pl.program_id(0) == 0) & (lt == 0))
    def _():
        kring_ref[...] = jnp.zeros_like(kring_ref)
        vtring_ref[...] = jnp.zeros_like(vtring_ref)

    slot = lambda back: (lt + (KV_BLOCKS - back)) % KV_BLOCKS
    _proj_body(lt == 0, x1_ref.at[0], nm_ref, win_ref, wvt_ref, vones_ref, cw_ref, cb_ref, dtb_ref, a_ref,
               xs_ref.at[0], bc_ref.at[0], z_ref.at[0], q_ref.at[0], kring_ref.at[slot(0)],
               vtring_ref.at[slot(0)], dtc_ref.at[0], acsc_ref.at[0], acst_ref, buf_ref)
    keys = [kring_ref.at[pl.ds(slot(back), 1)] for back in (2, 1, 0)]
    vts = [vtring_ref.at[slot(back)] for back in (2, 1, 0)]
    _attn_body(q_ref, *keys, *vts, ring_ref, o_ref, st_ref, bias_ref)
    _ssd_body(xs_ref, bc_ref, dtc_ref, acsc_ref, acst_ref, z_ref, dsk_ref, nw_ref, y_ref, h_ref)


def _front(x1, nm, win, wvt, vones, cw, cb, dtb, a, dsk, nw, ring):
    assert T_SSD == TQ_ATT
    bsz, seqlen, _ = x1.shape
    t = T_SSD
    tok = lambda n: pl.BlockSpec((1, t, n), lambda b, i: (b, i, 0))
    bc_dim = CONV_DIM - SSD_INNER
    return pl.pallas_call(
        _front_kernel,
        grid=(bsz, seqlen // t),
        in_specs=[tok(D_MODEL), _const_spec((1, D_MODEL)), _const_spec((D_MODEL, PROJ_PAD)),
                  _const_spec((VT_ROWS, D_MODEL)), _const_spec((VT_ROWS, 1)),
                  _const_spec((CONV_WIDTH, CONV_DIM)), _const_spec((1, CONV_DIM)),
                  _const_spec((DT_PAD, 1)), _const_spec((DT_PAD, 1)), _const_spec((1, SSD_INNER)),
                  _const_spec((1, SSD_INNER)), _const_spec((ATT_HEADS, ATT_RING))],
        out_specs=[tok(SSD_INNER), tok(ATT_INNER)],
        out_shape=[jax.ShapeDtypeStruct((bsz, seqlen, SSD_INNER), BF16),
                   jax.ShapeDtypeStruct((bsz, seqlen, ATT_INNER), BF16)],
        scratch_shapes=[pltpu.VMEM((CONV_DIM // LANES, t + SUBLANES, LANES), F32),
                        pltpu.VMEM((SSD_GROUPS, SSD_STATE, HEADS_PER_GROUP * SSD_HEAD_DIM), F32),
                        pltpu.VMEM((ATT_HEADS // 2, KV_BLOCKS, t, 2 * t), F32),
                        pltpu.VMEM((KV_BLOCKS, ATT_HEADS, KV_BLOCKS * t, t), F32),
                        pltpu.VMEM((KV_BLOCKS, t, ATT_INNER), BF16),
                        pltpu.VMEM((KV_BLOCKS, VT_ROWS, t), BF16),
                        pltpu.VMEM((1, t, SSD_INNER), F32), pltpu.VMEM((1, t, bc_dim), BF16),
                        pltpu.VMEM((1, t, SSD_INNER), F32), pltpu.VMEM((1, t, ATT_INNER), BF16),
                        pltpu.VMEM((1, t, DT_PAD), F32), pltpu.VMEM((1, t, DT_PAD), F32),
                        pltpu.VMEM((DT_PAD, t), F32)],
        compiler_params=pltpu.CompilerParams(dimension_semantics=("arbitrary", "arbitrary"),
                                             vmem_limit_bytes=VMEM_LIMIT),
        name="front",
    )(x1, nm, win, wvt, vones, cw, cb, dtb, a, dsk, nw, ring)


def _out_ffn2_kernel(x1_ref, y_ref, att_ref, wo_ref, n2_ref, wg_ref, wu_ref, wd_ref, nf_ref,
                     out_ref, act_ref):
    mix = jnp.dot(y_ref[...], wo_ref[0:SSD_INNER, :], preferred_element_type=F32)
    mix = mix + jnp.dot(att_ref[...], wo_ref[SSD_INNER:MIX_WIDTH, :], preferred_element_type=F32)
    x2 = x1_ref[...] + mix
    h = _rms(x2, n2_ref[...]).astype(BF16)
    x3 = x2 + 0.5 * _swiglu(h, wg_ref, wu_ref, wd_ref, act_ref)
    out_ref[...] = _rms(x3, nf_ref[...])


def _out_ffn2(x1, y, att, wo, n2, wg, wu, wd, nf):
    m = x1.shape[0]
    tm = TM_FFN2
    row = lambda n: pl.BlockSpec((tm, n), lambda i: (i, 0))
    return pl.pallas_call(
        _out_ffn2_kernel,
        grid=(m // tm,),
        in_specs=[row(D_MODEL), row(SSD_INNER), row(ATT_INNER), _const_spec((MIX_WIDTH, D_MODEL)),
                  _const_spec((1, D_MODEL)), _const_spec((D_MODEL, D_FF)), _const_spec((D_MODEL, D_FF)),
                  _const_spec((D_FF, D_MODEL)), _const_spec((1, D_MODEL))],
        out_specs=row(D_MODEL),
        out_shape=jax.ShapeDtypeStruct((m, D_MODEL), F32),
        scratch_shapes=[pltpu.VMEM((tm, D_FF), BF16)],
        compiler_params=pltpu.CompilerParams(dimension_semantics=("parallel",),
                                             vmem_limit_bytes=VMEM_LIMIT),
        name="out_ffn2",
    )(x1, y, att, wo, n2, wg, wu, wd, nf)


def _layer(x, ffn1_norm, ffn1_w_gate, ffn1_w_up, ffn1_w_down, mix_norm, w_in, conv_w, conv_b, dt_bias,
           a_log, d_skip, ssd_norm, rel_bias, w_out, ffn2_norm, ffn2_w_gate, ffn2_w_up, ffn2_w_down,
           out_norm):
    bsz, seqlen, _ = x.shape
    m = bsz * seqlen
    o_xbc = SSD_INNER
    o_dt = o_xbc + CONV_DIM
    o_q = o_dt + SSD_HEADS
    o_k = o_q + ATT_INNER
    o_v = o_k + ATT_INNER
    row = lambda p: p.reshape(1, -1).astype(F32)
    pad_col = lambda p: jnp.pad(p.astype(F32), (0, DT_PAD - SSD_HEADS)).reshape(DT_PAD, 1)

    x1, (wt_bf, wo_bf, wg2_bf, wu2_bf, wd2_bf) = _ffn1(
        x.reshape(m, D_MODEL), row(ffn1_norm), ffn1_w_gate.astype(BF16), ffn1_w_up.astype(BF16),
        ffn1_w_down.astype(BF16), [w_in.T, w_out, ffn2_w_gate, ffn2_w_up, ffn2_w_down])

    win = jnp.concatenate(
        [wt_bf[o_xbc:o_dt], wt_bf[:o_xbc], wt_bf[o_q:o_k] * ATT_HEAD_DIM ** -0.5, wt_bf[o_k:o_v],
         wt_bf[o_dt:o_q], jnp.zeros((DT_PAD - SSD_HEADS, D_MODEL), BF16)], axis=0).T
    pad_rows = ((0, 0), (0, VT_HEAD_ROWS - ATT_HEAD_DIM), (0, 0))
    wvt = jnp.pad(wt_bf[o_v:].reshape(ATT_HEADS, ATT_HEAD_DIM, D_MODEL), pad_rows)
    wvt = wvt.reshape(VT_ROWS, D_MODEL)
    vones = jnp.zeros((ATT_HEADS, VT_HEAD_ROWS), F32).at[:, ATT_HEAD_DIM].set(1.0).reshape(VT_ROWS, 1)

    a_log2 = -jnp.exp(a_log.astype(F32)) * math.log2(math.e)
    y, att = _front(x1.reshape(bsz, seqlen, D_MODEL), row(mix_norm), win, wvt, vones, conv_w.astype(F32),
                    row(conv_b), pad_col(dt_bias), pad_col(a_log2), row(jnp.repeat(d_skip, SSD_HEAD_DIM)),
                    row(ssd_norm), _attn_bias_ring(rel_bias))

    out = _out_ffn2(x1, y.reshape(m, SSD_INNER), att.reshape(m, ATT_INNER), wo_bf, row(ffn2_norm), wg2_bf,
                    wu2_bf, wd2_bf, row(out_norm))
    return out.reshape(bsz, seqlen, D_MODEL)


def kernel(x, ffn1_norm, ffn1_w_gate, ffn1_w_up, ffn1_w_down, mix_norm, w_in, conv_w, conv_b, dt_bias, a_log, d_skip, ssd_norm, rel_bias, w_out, ffn2_norm, ffn2_w_gate, ffn2_w_up, ffn2_w_down, final_norm):
    depth = ffn1_norm.shape[0]
    assert depth == 1, "the fused out_ffn2 kernel applies the final norm after the only layer"
    return _layer(x, ffn1_norm[0], ffn1_w_gate[0], ffn1_w_up[0], ffn1_w_down[0], mix_norm[0], w_in[0],
                  conv_w[0], conv_b[0], dt_bias[0], a_log[0], d_skip[0], ssd_norm[0], rel_bias[0],
                  w_out[0], ffn2_norm[0], ffn2_w_gate[0], ffn2_w_up[0], ffn2_w_down[0], final_norm)
```

```python
import functools
import math

import jax
import jax.numpy as jnp
import numpy as np
from jax import lax
from jax.experimental import pallas as pl
from jax.experimental.pallas import tpu as pltpu

D_MODEL = 1024
CHUNK = 64
LEFT_CHUNKS = 8
EPS = 1e-5
D_FF = 2816
SSD_HEADS = 16
SSD_HEAD_DIM = 64
SSD_INNER = SSD_HEADS * SSD_HEAD_DIM
SSD_GROUPS = 2
HEADS_PER_GROUP = SSD_HEADS // SSD_GROUPS
SSD_STATE = 128
CONV_WIDTH = 4
CONV_DIM = SSD_INNER + 2 * SSD_GROUPS * SSD_STATE
ATT_HEADS = 8
ATT_HEAD_DIM = 64
ATT_INNER = ATT_HEADS * ATT_HEAD_DIM
MAX_REL = 256
MIX_WIDTH = SSD_INNER + ATT_INNER

LANES = 128
SUBLANES = 8
DT_PAD = LANES
PROJ_PAD = CONV_DIM + SSD_INNER + 2 * ATT_INNER + DT_PAD
BF16_SUBLANES = 16
VT_HEAD_ROWS = ATT_HEAD_DIM + BF16_SUBLANES
VT_ROWS = ATT_HEADS * VT_HEAD_ROWS

TM_FFN1 = 1024
TM_FFN2 = 1024
T_SSD = 256
TQ_ATT = 256
KV_BLOCKS = 3
ATT_RING = (KV_BLOCKS + 1) * TQ_ATT
DEAD_QUADRANT = {0: (0, 1), KV_BLOCKS - 1: (1, 0)}
F_CHUNKS = ((0, 1024), (1024, 2048), (2048, D_FF))
VMEM_LIMIT = 56 * 1024 * 1024

F32 = jnp.float32
BF16 = jnp.bfloat16


def _rms(x, w):
    return x * lax.rsqrt(jnp.mean(x * x, axis=-1, keepdims=True) + EPS) * w


def _silu(x):
    return x * (1.0 / (1.0 + jnp.exp(-x)))


def _swiglu(h, wg_ref, wu_ref, wd_ref, act_ref):
    for lo, hi in F_CHUNKS:
        g = jnp.dot(h, wg_ref[:, lo:hi], preferred_element_type=F32)
        u = jnp.dot(h, wu_ref[:, lo:hi], preferred_element_type=F32)
        act_ref[:, lo:hi] = (_silu(g) * u).astype(BF16)
    return jnp.dot(act_ref[...], wd_ref[...], preferred_element_type=F32)


def _ffn1_kernel(n_cast, x_ref, n1_ref, wg_ref, wu_ref, wd_ref, *refs):
    cast_in, x1_ref, cast_out, act_ref = refs[:n_cast], refs[n_cast], refs[n_cast + 1:-1], refs[-1]
    x = x_ref[...]
    h = _rms(x, n1_ref[...]).astype(BF16)
    x1_ref[...] = x + 0.5 * _swiglu(h, wg_ref, wu_ref, wd_ref, act_ref)
    for src, dst in zip(cast_in, cast_out):
        dst[...] = src[...].astype(BF16)


def _cast_block_rows(rows, steps):
    return -(-rows // (steps * BF16_SUBLANES)) * BF16_SUBLANES


def _ffn1(x, n1, wg, wu, wd, to_cast):
    m = x.shape[0]
    tm = TM_FFN1
    steps = m // tm
    row = lambda n: pl.BlockSpec((tm, n), lambda i: (i, 0))

    def cast_spec(w):
        rb = _cast_block_rows(w.shape[0], steps)
        last = -(-w.shape[0] // rb) - 1
        return pl.BlockSpec((rb, w.shape[1]), lambda i: (jnp.minimum(i, last), 0))

    cast_specs = [cast_spec(w) for w in to_cast]
    outs = pl.pallas_call(
        functools.partial(_ffn1_kernel, len(to_cast)),
        grid=(steps,),
        in_specs=[row(D_MODEL), _const_spec((1, D_MODEL)), _const_spec((D_MODEL, D_FF)),
                  _const_spec((D_MODEL, D_FF)), _const_spec((D_FF, D_MODEL))] + cast_specs,
        out_specs=[row(D_MODEL)] + cast_specs,
        out_shape=[jax.ShapeDtypeStruct((m, D_MODEL), F32)]
                  + [jax.ShapeDtypeStruct(w.shape, BF16) for w in to_cast],
        scratch_shapes=[pltpu.VMEM((tm, D_FF), BF16)],
        compiler_params=pltpu.CompilerParams(dimension_semantics=("arbitrary",),
                                             vmem_limit_bytes=VMEM_LIMIT),
        name="ffn1",
    )(x, n1, wg, wu, wd, *to_cast)
    return outs[0], outs[1:]


def _proj_body(first_of_seq, x1_ref, nm_ref, win_ref, wvt_ref, vones_ref, cw_ref, cb_ref, dtb_ref,
               a_ref, xs_ref, bc_ref, z_ref, q_ref, k_ref, vt_ref, dtc_ref, acsc_ref, acst_ref, buf_ref):
    tm = T_SSD

    @pl.when(first_of_seq)
    def _():
        buf_ref[:, 0:SUBLANES, :] = jnp.zeros((CONV_DIM // LANES, SUBLANES, LANES), F32)

    @pl.when(jnp.logical_not(first_of_seq))
    def _():
        buf_ref[:, 0:SUBLANES, :] = buf_ref[:, tm:tm + SUBLANES, :]

    h2 = _rms(x1_ref[...], nm_ref[...]).astype(BF16)

    o = PROJ_PAD - DT_PAD
    dtc_ref[...] = jnp.dot(h2, win_ref[:, o:o + DT_PAD], preferred_element_type=F32)
    xr = dtc_ref[...].T + dtb_ref[...]
    dt_t = jnp.maximum(xr, 0.0) + jnp.log1p(jnp.exp(-jnp.abs(xr)))
    acs_t = dt_t * a_ref[...]
    pos = lax.broadcasted_iota(jnp.int32, (DT_PAD, tm), 1)
    shift = 1
    while shift < tm:
        acs_t = acs_t + jnp.where(pos >= shift, pltpu.roll(acs_t, shift, 1), 0.0)
        shift *= 2
    acst_ref[...] = acs_t
    dtc_ref[...] = dt_t.T
    acsc_ref[...] = acs_t.T

    raw = jnp.dot(h2, win_ref[:, 0:CONV_DIM], preferred_element_type=F32)
    for c in range(CONV_DIM // LANES):
        buf_ref[c, SUBLANES:SUBLANES + tm, :] = raw[:, c * LANES:(c + 1) * LANES]
    vt = lax.dot_general(wvt_ref[...], h2, (((1,), (1,)), ((), ())), preferred_element_type=F32)
    vt_ref[...] = (vt + vones_ref[...]).astype(BF16)
    o = CONV_DIM
    for ref in (z_ref, q_ref, k_ref):
        n = ref.shape[-1]
        ref[...] = jnp.dot(h2, win_ref[:, o:o + n], preferred_element_type=F32).astype(ref.dtype)
        o += n

    for c in range(CONV_DIM // LANES):
        lanes = slice(c * LANES, (c + 1) * LANES)
        acc = cb_ref[:, lanes]
        for k in range(CONV_WIDTH):
            off = SUBLANES - (CONV_WIDTH - 1) + k
            acc = acc + buf_ref[c, off:off + tm, :] * cw_ref[k:k + 1, lanes]
        if c < SSD_INNER // LANES:
            xs_ref[:, lanes] = _silu(acc)
        else:
            bc_ref[:, c * LANES - SSD_INNER:(c + 1) * LANES - SSD_INNER] = _silu(acc).astype(BF16)


def _const_spec(shape):
    return pl.BlockSpec(shape, lambda *_: (0,) * len(shape), pipeline_mode=pl.Buffered(1))


def _ssd_body(xs_ref, bc_ref, dtc_ref, acsc_ref, acst_ref, z_ref, dsk_ref, nw_ref, y_ref, h_ref):
    t = T_SSD
    hb = t // 2

    @pl.when(pl.program_id(1) == 0)
    def _():
        h_ref[...] = jnp.zeros_like(h_ref)

    gn = SSD_GROUPS * SSD_STATE
    xs = xs_ref[0]
    bm_bf = bc_ref[0, :, 0:gn]
    cm_bf = bc_ref[0, :, gn:2 * gn]
    dt_c = dtc_ref[0]
    acs_c = acsc_ref[0]
    acs_t = acst_ref[...]

    causal = (lax.broadcasted_iota(jnp.int32, (hb, hb), 0) >= lax.broadcasted_iota(jnp.int32, (hb, hb), 1))
    first = lambda rows: lax.broadcasted_iota(jnp.int32, (rows, 2 * SSD_HEAD_DIM), 1) < SSD_HEAD_DIM
    lo, lo_st, lo_row = first(t), first(SSD_STATE), first(1)
    pw = 2 * SSD_HEAD_DIM
    ys = []
    for g in range(SSD_GROUPS):
        n0 = g * SSD_STATE
        cg = cm_bf[:, n0:n0 + SSD_STATE]
        cbm = lax.dot_general(cg, bm_bf[:, n0:n0 + SSD_STATE], (((1,), (1,)), ((), ())),
                              preferred_element_type=F32)
        cbm_bf = cbm.astype(BF16)
        bt = bm_bf[:, n0:n0 + SSD_STATE].astype(F32).T.astype(BF16)
        hprev = h_ref[g]
        yoff = jnp.dot(cg, hprev.astype(BF16), preferred_element_type=F32)
        for pr in range(HEADS_PER_GROUP // 2):
            h0 = g * HEADS_PER_GROUP + 2 * pr
            h1 = h0 + 1
            l0 = pr * pw
            x_pair = xs[:, h0 * SSD_HEAD_DIM:h0 * SSD_HEAD_DIM + pw]
            xdt = (x_pair * jnp.where(lo, dt_c[:, h0:h0 + 1], dt_c[:, h1:h1 + 1])).astype(BF16)
            lhs = []
            for hd in (h0, h1):
                col = acs_c[:, hd:hd + 1]
                row = acs_t[hd:hd + 1, :]
                d00 = jnp.exp2(jnp.where(causal, col[:hb] - row[:, :hb], -jnp.inf).astype(BF16))
                d10 = jnp.exp2((col[hb:] - row[:, :hb]).astype(BF16))
                d11 = jnp.exp2(jnp.where(causal, col[hb:] - row[:, hb:], -jnp.inf).astype(BF16))
                top = jnp.concatenate([cbm_bf[:hb, :hb] * d00, jnp.zeros((hb, hb), BF16)], axis=1)
                bot = jnp.concatenate([cbm_bf[hb:, :hb] * d10, cbm_bf[hb:, hb:] * d11], axis=1)
                lhs += [top, bot]
            lasts = [acs_t[hd:hd + 1, t - 1:t] for hd in (h0, h1)]
            for hd, last in zip((h0, h1), lasts):
                lhs.append(bt * jnp.exp2((last - acs_t[hd:hd + 1, :]).astype(BF16)))
            r = jnp.dot(jnp.concatenate(lhs, axis=0), xdt, preferred_element_type=F32)
            yd = jnp.where(lo, r[0:t], r[t:2 * t])
            st = jnp.where(lo_st, r[2 * t:2 * t + SSD_STATE], r[2 * t + SSD_STATE:])
            ecol = jnp.exp2(jnp.where(lo, acs_c[:, h0:h0 + 1], acs_c[:, h1:h1 + 1]))
            ys.append(yd + yoff[:, l0:l0 + pw] * ecol)
            elast = jnp.exp2(jnp.where(lo_row, lasts[0], lasts[1]))
            h_ref[g, :, l0:l0 + pw] = hprev[:, l0:l0 + pw] * elast + st

    y = jnp.concatenate(ys, axis=1) + xs * dsk_ref[...]
    y = y * _silu(z_ref[0])
    y_ref[0] = _rms(y, nw_ref[...]).astype(y_ref.dtype)


def _attn_body(q_ref, k0_ref, k1_ref, k2_ref, vt0_ref, vt1_ref, vt2_ref, ring_ref, o_ref, st_ref,
               bias_ref):
    tq = TQ_ATT
    nk = KV_BLOCKS * tq
    lt = pl.program_id(1)

    @pl.when((pl.program_id(0) == 0) & (lt == 0))
    def _():
        kpos = lax.broadcasted_iota(jnp.int32, (nk, tq), 0)
        qpos = lax.broadcasted_iota(jnp.int32, (nk, tq), 1) + (KV_BLOCKS - 1) * tq
        dchunk = qpos // CHUNK - kpos // CHUNK
        neg = jnp.finfo(F32).min
        for hd in range(ATT_HEADS):
            ring = jnp.broadcast_to(ring_ref[hd:hd + 1, :], (nk, ATT_RING))
            toep = pltpu.roll(ring, 0, 1, stride=1, stride_axis=0)[:, 0:tq]
            band = jnp.where(dchunk >= 0, jnp.where(dchunk <= LEFT_CHUNKS, toep, neg), neg)
            for n in range(KV_BLOCKS):
                bias_ref[n, hd] = jnp.where(kpos >= (KV_BLOCKS - 1 - n) * tq, band, neg)

    case = jnp.minimum(lt, KV_BLOCKS - 1)
    pw = 2 * ATT_HEAD_DIM
    lo = lax.broadcasted_iota(jnp.int32, (tq, pw), 1) < ATT_HEAD_DIM
    nt = (((1,), (1,)), ((), ()))
    hk, hq = tq // 2, tq // 2
    for hp in range(ATT_HEADS // 2):
        c0 = hp * pw
        qp = q_ref[0, :, c0:c0 + pw]
        zero = jnp.zeros_like(qp)
        qs = jnp.concatenate([jnp.where(lo, qp, zero), jnp.where(lo, zero, qp)], axis=0)
        for j, kr in enumerate((k0_ref, k1_ref, k2_ref)):
            st_ref[hp, j] = lax.dot_general(kr[0, :, c0:c0 + pw], qs, nt, preferred_element_type=F32)
    for hp in range(ATT_HEADS // 2):
        c0 = hp * pw
        outs = []
        for e in range(2):
            hd = 2 * hp + e
            m = None
            ot = None
            for j, vr in reversed(list(enumerate((vt0_ref, vt1_ref, vt2_ref)))):
                def scores(r0, r1, q0, q1):
                    return (st_ref[hp, j, r0:r1, e * tq + q0:e * tq + q1]
                            + bias_ref[case, hd, j * tq + r0:j * tq + r1, q0:q1])
                if j not in DEAD_QUADRANT:
                    s = scores(0, tq, 0, tq)
                    mj = jnp.max(s, axis=0, keepdims=True)
                    m_new = mj if m is None else jnp.maximum(m, mj)
                    p = jnp.exp((s - m_new).astype(BF16))
                else:
                    rh, qh = DEAD_QUADRANT[j]
                    dead_q = (KV_BLOCKS - 1) * tq + qh * hq + np.arange(hq)
                    dead_k = j * tq + rh * hk + np.arange(hk)
                    dist = dead_q[None, :] // CHUNK - dead_k[:, None] // CHUNK
                    assert not np.any((dist >= 0) & (dist <= LEFT_CHUNKS)), "quadrant is inside the band"
                    full = scores((1 - rh) * hk, (2 - rh) * hk, 0, tq)
                    part = scores(rh * hk, (rh + 1) * hk, (1 - qh) * hq, (2 - qh) * hq)
                    mf = jnp.max(full, axis=0, keepdims=True)
                    mp = jnp.max(part, axis=0, keepdims=True)
                    neg = jnp.full((1, hq), jnp.finfo(F32).min, F32)
                    mp = jnp.concatenate([neg, mp] if qh == 0 else [mp, neg], axis=1)
                    mj = jnp.maximum(mf, mp)
                    m_new = mj if m is None else jnp.maximum(m, mj)
                    pf = jnp.exp((full - m_new).astype(BF16))
                    pp = jnp.exp((part - m_new[:, (1 - qh) * hq:(2 - qh) * hq]).astype(BF16))
                    zeros = jnp.zeros((hk, hq), BF16)
                    pp = jnp.concatenate([zeros, pp] if qh == 0 else [pp, zeros], axis=1)
                    p = jnp.concatenate([pp, pf] if rh == 0 else [pf, pp], axis=0)
                oj = jnp.dot(vr[hd * VT_HEAD_ROWS:(hd + 1) * VT_HEAD_ROWS, :], p,
                             preferred_element_type=F32)
                ot = oj if ot is None else ot * jnp.exp(m - m_new) + oj
                m = m_new
            outs.append(ot[0:ATT_HEAD_DIM] / ot[ATT_HEAD_DIM:ATT_HEAD_DIM + 1])
        o_ref[0, :, c0:c0 + pw] = jnp.concatenate(outs, axis=0).T.astype(o_ref.dtype)


def _attn_bias_ring(rel_bias):
    q0 = (KV_BLOCKS - 1) * TQ_ATT
    j = np.arange(ATT_RING)
    e = np.where(j <= TQ_ATT, -j, ATT_RING - j)
    idx = np.clip(q0 - e, -MAX_REL, MAX_REL) + MAX_REL
    lo, hi = q0 - MAX_REL + 1, ATT_RING - (q0 - MAX_REL)
    assert np.all(idx[:lo] == 2 * MAX_REL) and np.all(idx[hi:] == 2 * MAX_REL)
    assert np.all(idx[lo:hi] == np.arange(1, hi - lo + 1))
    rb = rel_bias.astype(F32)
    far = rb[:, 2 * MAX_REL:]
    return jnp.concatenate([jnp.repeat(far, lo, axis=1), rb[:, 1:hi - lo + 1],
                            jnp.repeat(far, ATT_RING - hi, axis=1)], axis=1)


def _front_kernel(x1_ref, nm_ref, win_ref, wvt_ref, vones_ref, cw_ref, cb_ref, dtb_ref, a_ref, dsk_ref,
                  nw_ref, ring_ref, y_ref, o_ref, buf_ref, h_ref, st_ref, bias_ref, kring_ref, vtring_ref,
                  xs_ref, bc_ref, z_ref, q_ref, dtc_ref, acsc_ref, acst_ref):
    lt = pl.program_id(1)

    @pl.when((pl.program_id(0) == 0) & (lt == 0))
    def _():
        kring_ref[...] = jnp.zeros_like(kring_ref)
        vtring_ref[...] = jnp.zeros_like(vtring_ref)

    slot = lambda back: (lt + (KV_BLOCKS - back)) % KV_BLOCKS
    _proj_body(lt == 0, x1_ref.at[0], nm_ref, win_ref, wvt_ref, vones_ref, cw_ref, cb_ref, dtb_ref, a_ref,
               xs_ref.at[0], bc_ref.at[0], z_ref.at[0], q_ref.at[0], kring_ref.at[slot(0)],
               vtring_ref.at[slot(0)], dtc_ref.at[0], acsc_ref.at[0], acst_ref, buf_ref)
    keys = [kring_ref.at[pl.ds(slot(back), 1)] for back in (2, 1, 0)]
    vts = [vtring_ref.at[slot(back)] for back in (2, 1, 0)]
    _attn_body(q_ref, *keys, *vts, ring_ref, o_ref, st_ref, bias_ref)
    _ssd_body(xs_ref, bc_ref, dtc_ref, acsc_ref, acst_ref, z_ref, dsk_ref, nw_ref, y_ref, h_ref)


def _front(x1, nm, win, wvt, vones, cw, cb, dtb, a, dsk, nw, ring):
    assert T_SSD == TQ_ATT
    bsz, seqlen, _ = x1.shape
    t = T_SSD
    tok = lambda n: pl.BlockSpec((1, t, n), lambda b, i: (b, i, 0))
    bc_dim = CONV_DIM - SSD_INNER
    return pl.pallas_call(
        _front_kernel,
        grid=(bsz, seqlen // t),
        in_specs=[tok(D_MODEL), _const_spec((1, D_MODEL)), _const_spec((D_MODEL, PROJ_PAD)),
                  _const_spec((VT_ROWS, D_MODEL)), _const_spec((VT_ROWS, 1)),
                  _const_spec((CONV_WIDTH, CONV_DIM)), _const_spec((1, CONV_DIM)),
                  _const_spec((DT_PAD, 1)), _const_spec((DT_PAD, 1)), _const_spec((1, SSD_INNER)),
                  _const_spec((1, SSD_INNER)), _const_spec((ATT_HEADS, ATT_RING))],
        out_specs=[tok(SSD_INNER), tok(ATT_INNER)],
        out_shape=[jax.ShapeDtypeStruct((bsz, seqlen, SSD_INNER), BF16),
                   jax.ShapeDtypeStruct((bsz, seqlen, ATT_INNER), BF16)],
        scratch_shapes=[pltpu.VMEM((CONV_DIM // LANES, t + SUBLANES, LANES), F32),
                        pltpu.VMEM((SSD_GROUPS, SSD_STATE, HEADS_PER_GROUP * SSD_HEAD_DIM), F32),
                        pltpu.VMEM((ATT_HEADS // 2, KV_BLOCKS, t, 2 * t), F32),
                        pltpu.VMEM((KV_BLOCKS, ATT_HEADS, KV_BLOCKS * t, t), F32),
                        pltpu.VMEM((KV_BLOCKS, t, ATT_INNER), BF16),
                        pltpu.VMEM((KV_BLOCKS, VT_ROWS, t), BF16),
                        pltpu.VMEM((1, t, SSD_INNER), F32), pltpu.VMEM((1, t, bc_dim), BF16),
                        pltpu.VMEM((1, t, SSD_INNER), F32), pltpu.VMEM((1, t, ATT_INNER), BF16),
                        pltpu.VMEM((1, t, DT_PAD), F32), pltpu.VMEM((1, t, DT_PAD), F32),
                        pltpu.VMEM((DT_PAD, t), F32)],
        compiler_params=pltpu.CompilerParams(dimension_semantics=("arbitrary", "arbitrary"),
                                             vmem_limit_bytes=VMEM_LIMIT),
        name="front",
    )(x1, nm, win, wvt, vones, cw, cb, dtb, a, dsk, nw, ring)


def _out_ffn2_kernel(x1_ref, y_ref, att_ref, wo_ref, n2_ref, wg_ref, wu_ref, wd_ref, nf_ref,
                     out_ref, act_ref):
    mix = jnp.dot(y_ref[...], wo_ref[0:SSD_INNER, :], preferred_element_type=F32)
    mix = mix + jnp.dot(att_ref[...], wo_ref[SSD_INNER:MIX_WIDTH, :], preferred_element_type=F32)
    x2 = x1_ref[...] + mix
    h = _rms(x2, n2_ref[...]).astype(BF16)
    x3 = x2 + 0.5 * _swiglu(h, wg_ref, wu_ref, wd_ref, act_ref)
    out_ref[...] = _rms(x3, nf_ref[...])


def _out_ffn2(x1, y, att, wo, n2, wg, wu, wd, nf):
    m = x1.shape[0]
    tm = TM_FFN2
    row = lambda n: pl.BlockSpec((tm, n), lambda i: (i, 0))
    return pl.pallas_call(
        _out_ffn2_kernel,
        grid=(m // tm,),
        in_specs=[row(D_MODEL), row(SSD_INNER), row(ATT_INNER), _const_spec((MIX_WIDTH, D_MODEL)),
                  _const_spec((1, D_MODEL)), _const_spec((D_MODEL, D_FF)), _const_spec((D_MODEL, D_FF)),
                  _const_spec((D_FF, D_MODEL)), _const_spec((1, D_MODEL))],
        out_specs=row(D_MODEL),
        out_shape=jax.ShapeDtypeStruct((m, D_MODEL), F32),
        scratch_shapes=[pltpu.VMEM((tm, D_FF), BF16)],
        compiler_params=pltpu.CompilerParams(dimension_semantics=("parallel",),
                                             vmem_limit_bytes=VMEM_LIMIT),
        name="out_ffn2",
    )(x1, y, att, wo, n2, wg, wu, wd, nf)


def _layer(x, ffn1_norm, ffn1_w_gate, ffn1_w_up, ffn1_w_down, mix_norm, w_in, conv_w, conv_b, dt_bias,
           a_log, d_skip, ssd_norm, rel_bias, w_out, ffn2_norm, ffn2_w_gate, ffn2_w_up, ffn2_w_down,
           out_norm):
    bsz, seqlen, _ = x.shape
    m = bsz * seqlen
    o_xbc = SSD_INNER
    o_dt = o_xbc + CONV_DIM
    o_q = o_dt + SSD_HEADS
    o_k = o_q + ATT_INNER
    o_v = o_k + ATT_INNER
    row = lambda p: p.reshape(1, -1).astype(F32)
    pad_col = lambda p: jnp.pad(p.astype(F32), (0, DT_PAD - SSD_HEADS)).reshape(DT_PAD, 1)

    x1, (wt_bf, wo_bf, wg2_bf, wu2_bf, wd2_bf) = _ffn1(
        x.reshape(m, D_MODEL), row(ffn1_norm), ffn1_w_gate.astype(BF16), ffn1_w_up.astype(BF16),
        ffn1_w_down.astype(BF16), [w_in.T, w_out, ffn2_w_gate, ffn2_w_up, ffn2_w_down])

    win = jnp.concatenate(
        [wt_bf[o_xbc:o_dt], wt_bf[:o_xbc], wt_bf[o_q:o_k] * ATT_HEAD_DIM ** -0.5, wt_bf[o_k:o_v],
         wt_bf[o_dt:o_q], jnp.zeros((DT_PAD - SSD_HEADS, D_MODEL), BF16)], axis=0).T
    pad_rows = ((0, 0), (0, VT_HEAD_ROWS - ATT_HEAD_DIM), (0, 0))
    wvt = jnp.pad(wt_bf[o_v:].reshape(ATT_HEADS, ATT_HEAD_DIM, D_MODEL), pad_rows)
    wvt = wvt.reshape(VT_ROWS, D_MODEL)
    vones = jnp.zeros((ATT_HEADS, VT_HEAD_ROWS), F32).at[:, ATT_HEAD_DIM].set(1.0).reshape(VT_ROWS, 1)

    a_log2 = -jnp.exp(a_log.astype(F32)) * math.log2(math.e)
    y, att = _front(x1.reshape(bsz, seqlen, D_MODEL), row(mix_norm), win, wvt, vones, conv_w.astype(F32),
                    row(conv_b), pad_col(dt_bias), pad_col(a_log2), row(jnp.repeat(d_skip, SSD_HEAD_DIM)),
                    row(ssd_norm), _attn_bias_ring(rel_bias))

    out = _out_ffn2(x1, y.reshape(m, SSD_INNER), att.reshape(m, ATT_INNER), wo_bf, row(ffn2_norm), wg2_bf,
                    wu2_bf, wd2_bf, row(out_norm))
    return out.reshape(bsz, seqlen, D_MODEL)


def kernel(x, ffn1_norm, ffn1_w_gate, ffn1_w_up, ffn1_w_down, mix_norm, w_in, conv_w, conv_b, dt_bias, a_log, d_skip, ssd_norm, rel_bias, w_out, ffn2_norm, ffn2_w_gate, ffn2_w_up, ffn2_w_down, final_norm):
    depth = ffn1_norm.shape[0]
    assert depth == 1, "the fused out_ffn2 kernel applies the final norm after the only layer"
    return _layer(x, ffn1_norm[0], ffn1_w_gate[0], ffn1_w_up[0], ffn1_w_down[0], mix_norm[0], w_in[0],
                  conv_w[0], conv_b[0], dt_bias[0], a_log[0], d_skip[0], ssd_norm[0], rel_bias[0],
                  w_out[0], ffn2_norm[0], ffn2_w_gate[0], ffn2_w_up[0], ffn2_w_down[0], final_norm)
```
